```python
import jax, jax.numpy as jnp
from jax import lax
import numpy as np

D_MODEL = 1024
BATCH = 32
SEQ = 256
DEPTH = 4
DEC_BATCH = 2
DEC_SEQ = 4096
PAST_LEN = 512

GRID_W = 64
N_MIXERS = 4
N_FNET = (DEPTH + 3) // 4
N_ATTN = (DEPTH + 2) // 4
N_POOL = (DEPTH + 1) // 4
N_GMLP = DEPTH // 4
N_GROUPS = 4
GROUP_W = D_MODEL // N_GROUPS
HEAD_DIM = 128
N_Q_HEADS = D_MODEL // HEAD_DIM
N_KV_HEADS = 2
Q_PER_KV = N_Q_HEADS // N_KV_HEADS
Q_W = N_Q_HEADS * HEAD_DIM
KV_W = N_KV_HEADS * HEAD_DIM
QKV_W = Q_W + 2 * KV_W
Q_BLOCK = 128
ROPE_THETA = 10000.0
POOL_HALF = (1, 2, 4, 8)
CHUNK = 128
GMLP_W = 2 * D_MODEL
GMLP_GROUPS = 4
GMLP_GW = GMLP_W // GMLP_GROUPS
D_FF = ((8 * D_MODEL // 3 + 127) // 128) * 128
ALPHA = (2 * DEPTH) ** 0.25
BETA = (8 * DEPTH) ** -0.25
LN_EPS = 1e-6

kernel_name = "hybrid_diffusion_prefix_trunk_step"

F32 = jnp.float32


def _layer_norm(x, g=None, b=None):
    xf = x.astype(F32)
    mu = jnp.mean(xf, axis=-1, keepdims=True)
    var = jnp.mean(jnp.square(xf - mu), axis=-1, keepdims=True)
    y = (xf - mu) * lax.rsqrt(var + LN_EPS)
    if g is not None:
        y = y * g.astype(F32) + b.astype(F32)
    return y.astype(x.dtype)


def _rms_norm(x, g):
    xf = x.astype(F32)
    y = xf * lax.rsqrt(jnp.mean(jnp.square(xf), axis=-1, keepdims=True) + LN_EPS)
    return (y * g.astype(F32)).astype(x.dtype)


def _axial_rope(x):
    T = x.shape[1]
    rows = T // GRID_W
    row = jnp.repeat(jnp.arange(rows), GRID_W).astype(F32)
    col = jnp.tile(jnp.arange(GRID_W), rows).astype(F32)
    half = HEAD_DIM // 2
    quarter = half // 2
    inv = ROPE_THETA ** (-jnp.arange(quarter, dtype=F32) / quarter)

    def rot(xa, pos):
        ang = pos[:, None] * inv[None, :]
        cos = jnp.cos(ang)[None, :, None, :].astype(x.dtype)
        sin = jnp.sin(ang)[None, :, None, :].astype(x.dtype)
        x1, x2 = xa[..., :quarter], xa[..., quarter:]
        return jnp.concatenate([x1 * cos - x2 * sin, x2 * cos + x1 * sin], axis=-1)

    return jnp.concatenate([rot(x[..., :half], row), rot(x[..., half:], col)], axis=-1)


def _qkv(h, w_qkv, q_g, k_g):
    B, T, _ = h.shape
    qkv = h @ w_qkv
    q = qkv[..., :Q_W].reshape(B, T, N_Q_HEADS, HEAD_DIM)
    k = qkv[..., Q_W:Q_W + KV_W].reshape(B, T, N_KV_HEADS, HEAD_DIM)
    v = qkv[..., Q_W + KV_W:].reshape(B, T, N_KV_HEADS, HEAD_DIM)
    return _rms_norm(q, q_g), _rms_norm(k, k_g), v


def _attend(q, k, v):
    B, T = q.shape[:2]
    nb = T // Q_BLOCK
    qb = q.reshape(B, nb, Q_BLOCK, N_KV_HEADS, Q_PER_KV, HEAD_DIM).transpose(1, 0, 2, 3, 4, 5)
    scale = HEAD_DIM ** -0.5

    def block(qblk):
        s = jnp.einsum('bqkgd,bskd->bkgqs', qblk, k).astype(F32) * scale
        p = jax.nn.softmax(s, axis=-1).astype(v.dtype)
        return jnp.einsum('bkgqs,bskd->bqkgd', p, v)

    o = lax.map(block, qb)
    return o.transpose(1, 0, 2, 3, 4, 5).reshape(B, T, Q_W)


def _fourier_mix(h, w_o):
    B, T, D = h.shape
    hg = h.astype(F32).reshape(B, T, N_GROUPS, GROUP_W)
    f = jnp.fft.fft2(hg, axes=(1, 3), norm='ortho').real
    return f.astype(h.dtype).reshape(B, T, D) @ w_o


def _pool_mix(h, w_grp, scale):
    B, T, D = h.shape
    hg = h.reshape(B, T, N_GROUPS, GROUP_W)
    csum = jnp.concatenate([jnp.zeros((B, 1, N_GROUPS, GROUP_W), F32),
                            jnp.cumsum(hg.astype(F32), axis=1)], axis=1)
    half = jnp.array(POOL_HALF, jnp.int32)[None, :]
    t = jnp.arange(T, dtype=jnp.int32)[:, None]
    lo = jnp.clip(t - half, 0, T)
    hi = jnp.clip(t + half, 0, T)
    g = jnp.arange(N_GROUPS)[None, :]
    wsum = csum[:, hi, g] - csum[:, lo, g]
    mean = wsum / (hi - lo).astype(F32)[None, :, :, None]
    p = (mean - hg.astype(F32)).astype(h.dtype)
    y = jnp.einsum('btgc,gcd->btgd', p, w_grp).reshape(B, T, D)
    return y * scale


def _gmlp_mix(h, w_in, b_in, ln_g, ln_b, w_s, b_s, w_o):
    B, T, _ = h.shape
    z = jax.nn.gelu(h @ w_in + b_in)
    u, v = jnp.split(z, 2, axis=-1)
    v = _layer_norm(v, ln_g, ln_b)
    vc = v.reshape(B, T // CHUNK, CHUNK, GMLP_GROUPS, GMLP_GW)
    s = jnp.einsum('bnpgc,gqp->bnqgc', vc, w_s) + b_s.T[None, None, :, :, None]
    return (u * s.reshape(B, T, GMLP_W)) @ w_o


def _conv_ffn(h, w_up, conv_w, conv_b, w_down):
    a, b = jnp.split(h @ w_up, 2, axis=-1)
    ap = jnp.pad(a, ((0, 0), (1, 1), (0, 0)))
    a = ap[:, :-2] * conv_w[0] + ap[:, 1:-1] * conv_w[1] + ap[:, 2:] * conv_w[2] + conv_b
    return (jax.nn.gelu(a) * b) @ w_down


def setup_inputs(seed: int = 0) -> dict:
    key = jax.random.key(seed)
    ks = jax.random.split(key, 32)
    D = D_MODEL
    nrm = lambda k, s, sc: jax.random.normal(k, s, F32) * sc
    return {
        'x_prompt': nrm(ks[0], (BATCH, SEQ, D), 1.0),
        'x_sample': nrm(ks[1], (DEC_BATCH, DEC_SEQ, D), 1.0),
        'cache_k': nrm(ks[2], (DEC_BATCH, N_ATTN, PAST_LEN, N_KV_HEADS, HEAD_DIM), 1.0),
        'cache_v': nrm(ks[3], (DEC_BATCH, N_ATTN, PAST_LEN, N_KV_HEADS, HEAD_DIM), 1.0),
        'c': nrm(ks[4], (DEC_BATCH, D), 1.0),
        'c_ctx': nrm(ks[5], (D,), 1.0),
        'w_mod': nrm(ks[6], (DEPTH, D, 6 * D), 0.5 * D ** -0.5),
        'b_mod': nrm(ks[7], (DEPTH, 6 * D), 0.02),
        'ln_g': 1.0 + nrm(ks[8], (DEPTH, 2, D), 0.02),
        'ln_b': nrm(ks[9], (DEPTH, 2, D), 0.02),
        'ffn_w_up': nrm(ks[10], (DEPTH, D, 2 * D_FF), D ** -0.5),
        'ffn_conv_w': nrm(ks[11], (DEPTH, 3, D_FF), 3 ** -0.5),
        'ffn_conv_b': nrm(ks[12], (DEPTH, D_FF), 0.02),
        'ffn_w_down': nrm(ks[13], (DEPTH, D_FF, D), BETA * D_FF ** -0.5),
        'fnet_w_o': nrm(ks[14], (N_FNET, D, D), BETA * D ** -0.5),
        'attn_w_qkv': nrm(ks[15], (N_ATTN, D, QKV_W), D ** -0.5),
        'attn_q_norm': 1.0 + nrm(ks[16], (N_ATTN, HEAD_DIM), 0.02),
        'attn_k_norm': 1.0 + nrm(ks[17], (N_ATTN, HEAD_DIM), 0.02),
        'attn_w_o': nrm(ks[18], (N_ATTN, Q_W, D), BETA * Q_W ** -0.5),
        'pool_w': nrm(ks[19], (N_POOL, N_GROUPS, GROUP_W, GROUP_W), BETA * GROUP_W ** -0.5),
        'pool_scale': 1.0 + nrm(ks[20], (N_POOL, D), 0.1),
        'gmlp_w_in': nrm(ks[21], (N_GMLP, D, 2 * GMLP_W), D ** -0.5),
        'gmlp_b_in': nrm(ks[22], (N_GMLP, 2 * GMLP_W), 0.02),
        'gmlp_ln_g': 1.0 + nrm(ks[23], (N_GMLP, GMLP_W), 0.02),
        'gmlp_ln_b': nrm(ks[24], (N_GMLP, GMLP_W), 0.02),
        'gmlp_w_s': nrm(ks[25], (N_GMLP, GMLP_GROUPS, CHUNK, CHUNK), 0.5 * CHUNK ** -0.5),
        'gmlp_b_s': 1.0 + nrm(ks[26], (N_GMLP, GMLP_GROUPS, CHUNK), 0.02),
        'gmlp_w_o': nrm(ks[27], (N_GMLP, GMLP_W, D), BETA * GMLP_W ** -0.5),
    }


def reference(x_prompt, x_sample, cache_k, cache_v, c, c_ctx, w_mod, b_mod, ln_g, ln_b,
              ffn_w_up, ffn_conv_w, ffn_conv_b, ffn_w_down, fnet_w_o, attn_w_qkv,
              attn_q_norm, attn_k_norm, attn_w_o, pool_w, pool_scale, gmlp_w_in, gmlp_b_in,
              gmlp_ln_g, gmlp_ln_b, gmlp_w_s, gmlp_b_s, gmlp_w_o):
    x_p, x_s = x_prompt, x_sample
    cond_p = jax.nn.silu(c_ctx)[None, :]
    cond_s = jax.nn.silu(c)
    new_ks, new_vs = [], []

    for i in range(DEPTH):
        kind, j = i % N_MIXERS, i // N_MIXERS
        m_p = jnp.split((cond_p @ w_mod[i] + b_mod[i])[:, None, :], 6, axis=-1)
        m_s = jnp.split((cond_s @ w_mod[i] + b_mod[i])[:, None, :], 6, axis=-1)

        h_p = _layer_norm(x_p) * (1 + m_p[1]) + m_p[0]
        h_s = _layer_norm(x_s) * (1 + m_s[1]) + m_s[0]
        if kind == 0:
            y_p = _fourier_mix(h_p, fnet_w_o[j])
            y_s = _fourier_mix(h_s, fnet_w_o[j])
        elif kind == 1:
            q_p, k_p, v_p = _qkv(h_p, attn_w_qkv[j], attn_q_norm[j], attn_k_norm[j])
            y_p = _attend(q_p, k_p, v_p) @ attn_w_o[j]
            new_ks.append(k_p)
            new_vs.append(v_p)
            q_s, k_s, v_s = _qkv(h_s, attn_w_qkv[j], attn_q_norm[j], attn_k_norm[j])
            q_s, k_s = _axial_rope(q_s), _axial_rope(k_s)
            k_all = jnp.concatenate([k_s, cache_k[:, j]], axis=1)
            v_all = jnp.concatenate([v_s, cache_v[:, j]], axis=1)
            y_s = _attend(q_s, k_all, v_all) @ attn_w_o[j]
        elif kind == 2:
            y_p = _pool_mix(h_p, pool_w[j], pool_scale[j])
            y_s = _pool_mix(h_s, pool_w[j], pool_scale[j])
        else:
            gp = (gmlp_w_in[j], gmlp_b_in[j], gmlp_ln_g[j], gmlp_ln_b[j], gmlp_w_s[j],
                  gmlp_b_s[j], gmlp_w_o[j])
            y_p = _gmlp_mix(h_p, *gp)
            y_s = _gmlp_mix(h_s, *gp)
        x_p = _layer_norm(ALPHA * x_p + m_p[2] * y_p, ln_g[i, 0], ln_b[i, 0])
        x_s = _layer_norm(ALPHA * x_s + m_s[2] * y_s, ln_g[i, 0], ln_b[i, 0])

        h_p = _layer_norm(x_p) * (1 + m_p[4]) + m_p[3]
        h_s = _layer_norm(x_s) * (1 + m_s[4]) + m_s[3]
        f = (ffn_w_up[i], ffn_conv_w[i], ffn_conv_b[i], ffn_w_down[i])
        x_p = _layer_norm(ALPHA * x_p + m_p[5] * _conv_ffn(h_p, *f), ln_g[i, 1], ln_b[i, 1])
        x_s = _layer_norm(ALPHA * x_s + m_s[5] * _conv_ffn(h_s, *f), ln_g[i, 1], ln_b[i, 1])

    new_k = jnp.stack(new_ks, axis=1)
    new_v = jnp.stack(new_vs, axis=1)
    return (x_p, x_s, new_k, new_v)
```

```python
import functools

import numpy as np
import jax
import jax.numpy as jnp
from jax import lax
from jax.experimental import pallas as pl
from jax.experimental.pallas import tpu as pltpu

D_MODEL = 1024
BATCH = 32
SEQ = 256
DEPTH = 4
DEC_BATCH = 2
DEC_SEQ = 4096
PAST_LEN = 512
GRID_W = 64
N_GROUPS = 4
GROUP_W = D_MODEL // N_GROUPS
HEAD_DIM = 128
N_Q_HEADS = D_MODEL // HEAD_DIM
N_KV_HEADS = 2
Q_PER_KV = N_Q_HEADS // N_KV_HEADS
Q_W = N_Q_HEADS * HEAD_DIM
KV_W = N_KV_HEADS * HEAD_DIM
QKV_W = Q_W + 2 * KV_W
ROPE_THETA = 10000.0
POOL_HALF = (1, 2, 4, 8)
CHUNK = 128
GMLP_W = 2 * D_MODEL
GMLP_GROUPS = 4
GMLP_GW = GMLP_W // GMLP_GROUPS
D_FF = ((8 * D_MODEL // 3 + 127) // 128) * 128
ALPHA = (2 * DEPTH) ** 0.25
LN_EPS = 1e-6

N_P = BATCH * SEQ
N_S = DEC_BATCH * DEC_SEQ
N_TOK = N_P + N_S
N_COND = 1 + DEC_BATCH
COND_PAD = 8

F32 = jnp.float32
BF16 = jnp.bfloat16

VMEM_LIMIT_BYTES = 56 * 1024 * 1024
BF16_ROWS = 16

TM_SEQ = SEQ
TM_FFN = 512
TM_QKV = 512
FFN_CHUNK = D_FF // 2
ROPE_LANES = 128
N_TA = DEC_SEQ // ROPE_LANES


def _dot(a, b):
    return jnp.dot(a, b, preferred_element_type=F32)


def _dot_nt(a, b):
    return lax.dot_general(a, b, (((1,), (1,)), ((), ())), preferred_element_type=F32)


def _ln(x):
    mu = jnp.mean(x, axis=-1, keepdims=True)
    xc = x - mu
    var = jnp.mean(xc * xc, axis=-1, keepdims=True)
    return xc * lax.rsqrt(var + LN_EPS)


def _post_norm(x, y, gate, g, b):
    return _ln(ALPHA * x + gate * y) * g + b


def _cond_row(row_start):
    return jnp.maximum(row_start - (N_P - DEC_SEQ), 0) // DEC_SEQ


def _seq_pos(row_start):
    is_p = row_start < N_P
    length = jnp.where(is_p, SEQ, DEC_SEQ)
    pos = jnp.where(is_p, row_start, row_start - N_P) & (length - 1)
    return length, pos


def _params():
    return pltpu.CompilerParams(dimension_semantics=("arbitrary",), vmem_limit_bytes=VMEM_LIMIT_BYTES)


def _resident(shape):
    return pl.BlockSpec(shape, lambda i: (0,) * len(shape), pipeline_mode=pl.Buffered(1))


def _mod_spec(tm):
    return pl.BlockSpec((None, 6, D_MODEL), lambda i: (_cond_row(i * tm), 0, 0))


def _mod_kernel(c_ref, w_ref, b_ref, o_ref):
    c = c_ref[...]
    s = c / (1.0 + jnp.exp(-c))
    o_ref[...] = _dot(s.astype(BF16), w_ref[...].astype(BF16)) + b_ref[...]


def _modulation(cond, w_mod, b_mod):
    tn = 2048
    out = pl.pallas_call(
        _mod_kernel,
        grid=(DEPTH, 6 * D_MODEL // tn),
        in_specs=[
            pl.BlockSpec((COND_PAD, D_MODEL), lambda l, j: (0, 0)),
            pl.BlockSpec((None, D_MODEL, tn), lambda l, j: (l, 0, j)),
            pl.BlockSpec((None, 1, tn), lambda l, j: (l, 0, j)),
        ],
        out_specs=pl.BlockSpec((None, COND_PAD, tn), lambda l, j: (l, 0, j)),
        out_shape=jax.ShapeDtypeStruct((DEPTH, COND_PAD, 6 * D_MODEL), F32),
        compiler_params=pltpu.CompilerParams(
            dimension_semantics=("arbitrary", "arbitrary"), vmem_limit_bytes=VMEM_LIMIT_BYTES),
        name="modulation",
    )(cond, w_mod, b_mod.reshape(DEPTH, 1, 6 * D_MODEL))
    return out[:, :N_COND].reshape(DEPTH, N_COND, 6, D_MODEL)


def _ffn_kernel(x_ref, xp_ref, xn_ref, mod_ref, g_ref, b_ref, wup_ref, cw_ref, cb_ref, wdn_ref,
                o_ref, hext_ref, aext_ref, *, tm):
    halo = BF16_ROWS
    start = pl.program_id(0) * tm
    length, pos0 = _seq_pos(start)
    m = mod_ref[...]
    shift, scale, gate = m[3:4], m[4:5], m[5:6]

    def modulate(v):
        return (_ln(v) * (1.0 + scale) + shift).astype(BF16)

    x = x_ref[...]
    hext_ref[0:halo] = modulate(xp_ref[...])
    hext_ref[halo:halo + tm] = modulate(x)
    hext_ref[halo + tm:] = modulate(xn_ref[...])

    t = (pos0 + lax.broadcasted_iota(jnp.int32, (tm, 1), 0)) & (length - 1)
    has_prev = t != 0
    has_next = t != length - 1

    acc = jnp.zeros((tm, D_MODEL), F32)
    for c in range(D_FF // FFN_CHUNK):
        lo = c * FFN_CHUNK
        aext_ref[...] = _dot(hext_ref[...], wup_ref[:, lo:lo + FFN_CHUNK])
        bgate = _dot(hext_ref[halo:halo + tm], wup_ref[:, D_FF + lo:D_FF + lo + FFN_CHUNK])
        cw = cw_ref[:, lo:lo + FFN_CHUNK]
        a = (jnp.where(has_prev, aext_ref[halo - 1:halo - 1 + tm], 0.0) * cw[0:1]
             + aext_ref[halo:halo + tm] * cw[1:2]
             + jnp.where(has_next, aext_ref[halo + 1:halo + 1 + tm], 0.0) * cw[2:3]
             + cb_ref[:, lo:lo + FFN_CHUNK])
        act = (jax.nn.gelu(a) * bgate).astype(BF16)
        acc = acc + _dot(act, wdn_ref[lo:lo + FFN_CHUNK, :])
    o_ref[...] = _post_norm(x, acc, gate, g_ref[...], b_ref[...])


def _ffn(x, mod, ln_g, ln_b, w_up, conv_w, conv_b, w_down):
    tm, halo = TM_FFN, BF16_ROWS
    nb = N_TOK // halo
    return pl.pallas_call(
        functools.partial(_ffn_kernel, tm=tm),
        grid=(N_TOK // tm,),
        in_specs=[
            pl.BlockSpec((tm, D_MODEL), lambda i: (i, 0)),
            pl.BlockSpec((halo, D_MODEL), lambda i: (jnp.maximum(i * (tm // halo) - 1, 0), 0)),
            pl.BlockSpec((halo, D_MODEL), lambda i: (jnp.minimum((i + 1) * (tm // halo), nb - 1), 0)),
            _mod_spec(tm),
            _resident((1, D_MODEL)),
            _resident((1, D_MODEL)),
            _resident((D_MODEL, 2 * D_FF)),
            _resident((3, D_FF)),
            _resident((1, D_FF)),
            _resident((D_FF, D_MODEL)),
        ],
        out_specs=pl.BlockSpec((tm, D_MODEL), lambda i: (i, 0)),
        out_shape=jax.ShapeDtypeStruct((N_TOK, D_MODEL), F32),
        scratch_shapes=[
            pltpu.VMEM((tm + 2 * halo, D_MODEL), BF16),
            pltpu.VMEM((tm + 2 * halo, FFN_CHUNK), F32),
        ],
        compiler_params=_params(),
        name="conv_ffn",
    )(x, x, x, mod, ln_g.reshape(1, D_MODEL), ln_b.reshape(1, D_MODEL), w_up.astype(BF16), conv_w,
      conv_b.reshape(1, D_FF), w_down.astype(BF16))


def _dft_tables():
    k = np.arange(SEQ, dtype=np.int64)
    ang = 2.0 * np.pi * ((k[:, None] * k[None, :]) % SEQ) / SEQ
    c256, s256 = np.cos(ang), np.sin(ang)
    k = np.arange(DEC_SEQ, dtype=np.int64)
    tb = np.arange(ROPE_LANES, dtype=np.int64)
    ang_b = 2.0 * np.pi * ((k[:, None] * tb[None, :]) % DEC_SEQ) / DEC_SEQ
    ta = np.arange(N_TA, dtype=np.int64)
    ang_a = 2.0 * np.pi * ((k[:, None] * ta[None, :]) % N_TA) / N_TA
    pad = np.zeros((DEC_SEQ, ROPE_LANES - N_TA))
    f = lambda a: np.asarray(a, np.float32)
    return (f(c256), f(s256), f(np.cos(ang_b)), f(np.sin(ang_b)),
            f(np.concatenate([np.cos(ang_a), pad], 1)), f(np.concatenate([np.sin(ang_a), pad], 1)))


def _fnet_a_kernel(x_ref, mod_ref, cs_ref, p_ref, q_ref):
    m = mod_ref[...]
    h = (_ln(x_ref[...]) * (1.0 + m[1:2]) + m[0:1]).astype(BF16)
    for g in range(N_GROUPS):
        r = _dot(h[:, g * GROUP_W:(g + 1) * GROUP_W], cs_ref[...])
        p_ref[:, g * GROUP_W:(g + 1) * GROUP_W] = r[:, :GROUP_W].astype(BF16)
        q_ref[:, g * GROUP_W:(g + 1) * GROUP_W] = r[:, GROUP_W:].astype(BF16)


def _fnet_b_kernel(x_ref, ps_ref, qs_ref, pb_ref, qb_ref, c256_ref, s256_ref, cb_ref, sb_ref, ca_ref,
                   sa_ref, wo_ref, mod_ref, g_ref, b_ref, o_ref, ct_ref, st_ref, f_ref):
    i = pl.program_id(0)

    @pl.when(i < N_P // TM_SEQ)
    def _():
        f = _dot(c256_ref[...], ps_ref[...]) - _dot(s256_ref[...], qs_ref[...])
        f_ref[...] = (f * (SEQ * GROUP_W) ** -0.5).astype(BF16)

    @pl.when(i >= N_P // TM_SEQ)
    def _():
        cb, sb = cb_ref[...], sb_ref[...]
        ca, sa = ca_ref[...], sa_ref[...]
        for ta in range(N_TA):
            cca, ssa = ca[:, ta:ta + 1], sa[:, ta:ta + 1]
            ct_ref[:, ta * ROPE_LANES:(ta + 1) * ROPE_LANES] = (cca * cb - ssa * sb).astype(BF16)
            st_ref[:, ta * ROPE_LANES:(ta + 1) * ROPE_LANES] = (ssa * cb + cca * sb).astype(BF16)
        f = _dot(ct_ref[...], pb_ref[...]) - _dot(st_ref[...], qb_ref[...])
        f_ref[...] = (f * (DEC_SEQ * GROUP_W) ** -0.5).astype(BF16)

    y = _dot(f_ref[...], wo_ref[...])
    o_ref[...] = _post_norm(x_ref[...], y, mod_ref[...][2:3], g_ref[...], b_ref[...])


def _fnet(x, mod, ln_g, ln_b, w_o):
    c256, s256, cb, sb, ca, sa = _dft_tables()
    cs = jnp.concatenate([jnp.asarray(c256), jnp.asarray(s256)], axis=1).astype(BF16)
    tm = 512
    p, q = pl.pallas_call(
        _fnet_a_kernel,
        grid=(N_TOK // tm,),
        in_specs=[
            pl.BlockSpec((tm, D_MODEL), lambda i: (i, 0)),
            _mod_spec(tm),
            _resident((GROUP_W, 2 * GROUP_W)),
        ],
        out_specs=[pl.BlockSpec((tm, D_MODEL), lambda i: (i, 0))] * 2,
        out_shape=[jax.ShapeDtypeStruct((N_TOK, D_MODEL), BF16)] * 2,
        compiler_params=_params(),
        name="fnet_channel_dft",
    )(x, mod, cs)

    tm = TM_SEQ
    n_pt = N_P // tm
    per_seq = DEC_SEQ // tm
    small = pl.BlockSpec((tm, D_MODEL), lambda i: (jnp.minimum(i, n_pt - 1), 0))
    big = pl.BlockSpec((DEC_SEQ, D_MODEL),
                       lambda i: (N_P // DEC_SEQ + jnp.maximum(i - n_pt, 0) // per_seq, 0),
                       pipeline_mode=pl.Buffered(1))
    tab = pl.BlockSpec((tm, ROPE_LANES), lambda i: (jnp.maximum(i - n_pt, 0) % per_seq, 0))
    return pl.pallas_call(
        _fnet_b_kernel,
        grid=(N_TOK // tm,),
        in_specs=[
            pl.BlockSpec((tm, D_MODEL), lambda i: (i, 0)),
            small, small, big, big,
            _resident((SEQ, SEQ)), _resident((SEQ, SEQ)),
            tab, tab, tab, tab,
            _resident((D_MODEL, D_MODEL)),
            _mod_spec(tm),
            _resident((1, D_MODEL)), _resident((1, D_MODEL)),
        ],
        out_specs=pl.BlockSpec((tm, D_MODEL), lambda i: (i, 0)),
        out_shape=jax.ShapeDtypeStruct((N_TOK, D_MODEL), F32),
        scratch_shapes=[
            pltpu.VMEM((tm, DEC_SEQ), BF16),
            pltpu.VMEM((tm, DEC_SEQ), BF16),
            pltpu.VMEM((tm, D_MODEL), BF16),
        ],
        compiler_params=_params(),
        name="fnet_token_dft",
    )(x, p, q, p, q, jnp.asarray(c256).astype(BF16), jnp.asarray(s256).astype(BF16),
      jnp.asarray(cb), jnp.asarray(sb), jnp.asarray(ca), jnp.asarray(sa),
      w_o.astype(BF16), mod, ln_g.reshape(1, D_MODEL), ln_b.reshape(1, D_MODEL))


def _rope_tables(tm):
    half, quarter = HEAD_DIM // 2, HEAD_DIM // 4
    t = np.arange(DEC_SEQ)
    row, col = (t // GRID_W).astype(np.float32), (t % GRID_W).astype(np.float32)
    inv = (np.float32(ROPE_THETA) ** (-np.arange(quarter, dtype=np.float32) / np.float32(quarter))).astype(np.float32)
    ang_r = (row[:, None] * inv[None, :]).astype(np.float32).astype(np.float64)
    ang_c = (col[:, None] * inv[None, :]).astype(np.float32).astype(np.float64)
    cos = np.concatenate([np.cos(ang_r), np.cos(ang_r), np.cos(ang_c), np.cos(ang_c)], axis=1)
    sin = np.concatenate([-np.sin(ang_r), np.sin(ang_r), -np.sin(ang_c), np.sin(ang_c)], axis=1)
    cos = np.concatenate([cos, np.ones((tm, HEAD_DIM))], axis=0)
    sin = np.concatenate([sin, np.zeros((tm, HEAD_DIM))], axis=0)
    assert cos.shape == (DEC_SEQ + tm, 2 * half)
    return np.asarray(cos, np.float32), np.asarray(sin, np.float32)


def _qkv_kernel(x_ref, mod_ref, w_ref, qg_ref, kg_ref, cos_ref, sin_ref,
                q_ref, k_ref, v_ref, nk_ref, nv_ref, *, tm):
    i = pl.program_id(0)
    m = mod_ref[...]
    h = (_ln(x_ref[...]) * (1.0 + m[1:2]) + m[0:1]).astype(BF16)
    qkv = _dot(h, w_ref[...])
    cos, sin = cos_ref[...], sin_ref[...]
    lane = lax.broadcasted_iota(jnp.int32, (1, HEAD_DIM), 1)
    low = (lane % (HEAD_DIM // 2)) < (HEAD_DIM // 4)

    def rms(v, g):
        return v * lax.rsqrt(jnp.mean(v * v, axis=-1, keepdims=True) + LN_EPS) * g

    def rope(v):
        partner = jnp.where(low, pltpu.roll(v, HEAD_DIM - HEAD_DIM // 4, 1), pltpu.roll(v, HEAD_DIM // 4, 1))
        return v * cos + partner * sin

    is_prompt = i < N_P // tm
    for hq in range(N_Q_HEADS):
        sl = slice(hq * HEAD_DIM, (hq + 1) * HEAD_DIM)
        q_ref[:, sl] = rope(rms(qkv[:, sl], qg_ref[...])).astype(BF16)
    for hk in range(N_KV_HEADS):
        sl = slice(hk * HEAD_DIM, (hk + 1) * HEAD_DIM)
        kn = rms(qkv[:, Q_W + hk * HEAD_DIM:Q_W + (hk + 1) * HEAD_DIM], kg_ref[...])
        k_ref[:, sl] = rope(kn).astype(BF16)

        @pl.when(is_prompt)
        def _():
            nk_ref[:, sl] = kn

    v = qkv[:, Q_W + KV_W:]
    v_ref[...] = v.astype(BF16)

    @pl.when(is_prompt)
    def _():
        nv_ref[...] = v


def _softmax_pv(scores, values):
    mx = functools.reduce(jnp.maximum, [jnp.max(s, axis=-1, keepdims=True) for s in scores])
    ps = [jnp.exp(s - mx) for s in scores]
    den = functools.reduce(jnp.add, [jnp.sum(p, axis=-1, keepdims=True) for p in ps])
    num = functools.reduce(jnp.add, [_dot(p.astype(BF16), v) for p, v in zip(ps, values)])
    return num / den


def _attn_kernel(x_ref, q_ref, ks_ref, vs_ref, kb_ref, vb_ref, kc_ref, vc_ref, wo_ref, mod_ref, g_ref,
                 b_ref, o_ref, att_ref):
    i = pl.program_id(0)
    tm = TM_SEQ

    def head(hq):
        return slice(hq * HEAD_DIM, (hq + 1) * HEAD_DIM)

    @pl.when(i < N_P // tm)
    def _():
        for hk in range(N_KV_HEADS):
            q4 = jnp.concatenate([q_ref[:, head(hk * Q_PER_KV + g)] for g in range(Q_PER_KV)], axis=0)
            o4 = _softmax_pv([_dot_nt(q4, ks_ref[:, head(hk)])], [vs_ref[:, head(hk)]])
            for g in range(Q_PER_KV):
                att_ref[:, head(hk * Q_PER_KV + g)] = o4[g * tm:(g + 1) * tm].astype(BF16)

    @pl.when(i >= N_P // tm)
    def _():
        pair = 2
        for hk in range(N_KV_HEADS):
            for g0 in range(0, Q_PER_KV, pair):
                hq0 = hk * Q_PER_KV + g0
                qq = jnp.concatenate([q_ref[:, head(hq0 + g)] for g in range(pair)], axis=0)
                oo = _softmax_pv([_dot_nt(qq, kb_ref[:, head(hk)]), _dot_nt(qq, kc_ref[:, head(hk)])],
                                 [vb_ref[:, head(hk)], vc_ref[:, head(hk)]])
                for g in range(pair):
                    att_ref[:, head(hq0 + g)] = oo[g * tm:(g + 1) * tm].astype(BF16)

    y = _dot(att_ref[...], wo_ref[...])
    o_ref[...] = _post_norm(x_ref[...], y, mod_ref[...][2:3], g_ref[...], b_ref[...])


def _attention(x, mod, ln_g, ln_b, w_qkv, q_norm, k_norm, w_o, cache_k, cache_v):
    tm = TM_QKV
    n_pt = N_P // tm
    cos, sin = _rope_tables(tm)
    rope_spec = pl.BlockSpec(
        (tm, HEAD_DIM),
        lambda i: (jnp.where(i < n_pt, DEC_SEQ // tm, jnp.maximum(i - n_pt, 0) % (DEC_SEQ // tm)), 0))
    row = lambda w: pl.BlockSpec((tm, w), lambda i: (i, 0))
    prompt_row = pl.BlockSpec((tm, KV_W), lambda i: (jnp.minimum(i, n_pt - 1), 0))
    q, k, v, new_k, new_v = pl.pallas_call(
        functools.partial(_qkv_kernel, tm=tm),
        grid=(N_TOK // tm,),
        in_specs=[
            row(D_MODEL),
            _mod_spec(tm),
            _resident((D_MODEL, QKV_W)),
            _resident((1, HEAD_DIM)), _resident((1, HEAD_DIM)),
            rope_spec, rope_spec,
        ],
        out_specs=[row(Q_W), row(KV_W), row(KV_W), prompt_row, prompt_row],
        out_shape=[
            jax.ShapeDtypeStruct((N_TOK, Q_W), BF16),
            jax.ShapeDtypeStruct((N_TOK, KV_W), BF16),
            jax.ShapeDtypeStruct((N_TOK, KV_W), BF16),
            jax.ShapeDtypeStruct((N_P, KV_W), F32),
            jax.ShapeDtypeStruct((N_P, KV_W), F32),
        ],
        compiler_params=_params(),
        name="qkv_rope",
    )(x, mod, w_qkv.astype(BF16), (q_norm * HEAD_DIM ** -0.5).reshape(1, HEAD_DIM),
      k_norm.reshape(1, HEAD_DIM), jnp.asarray(cos), jnp.asarray(sin))

    tm = TM_SEQ
    n_pt = N_P // tm
    per_seq = DEC_SEQ // tm
    small = pl.BlockSpec((tm, KV_W), lambda i: (jnp.minimum(i, n_pt - 1), 0))
    big = pl.BlockSpec((DEC_SEQ, KV_W), lambda i: (N_P // DEC_SEQ + jnp.maximum(i - n_pt, 0) // per_seq, 0))
    cache = pl.BlockSpec((None, PAST_LEN, KV_W), lambda i: (jnp.maximum(i - n_pt, 0) // per_seq, 0, 0))
    x_new = pl.pallas_call(
        _attn_kernel,
        grid=(N_TOK // tm,),
        in_specs=[
            pl.BlockSpec((tm, D_MODEL), lambda i: (i, 0)),
            pl.BlockSpec((tm, Q_W), lambda i: (i, 0)),
            small, small, big, big, cache, cache,
            _resident((Q_W, D_MODEL)),
            _mod_spec(tm),
            _resident((1, D_MODEL)), _resident((1, D_MODEL)),
        ],
        out_specs=pl.BlockSpec((tm, D_MODEL), lambda i: (i, 0)),
        out_shape=jax.ShapeDtypeStruct((N_TOK, D_MODEL), F32),
        scratch_shapes=[pltpu.VMEM((tm, Q_W), BF16)],
        compiler_params=_params(),
        name="attention",
    )(x, q, k, v, k, v, cache_k.reshape(DEC_BATCH, PAST_LEN, KV_W).astype(BF16),
      cache_v.reshape(DEC_BATCH, PAST_LEN, KV_W).astype(BF16), w_o.astype(BF16), mod,
      ln_g.reshape(1, D_MODEL), ln_b.reshape(1, D_MODEL))
    return x_new, new_k, new_v


POOL_HALO = 8


def _pool_kernel(x_ref, xp_ref, xn_ref, mod_ref, w_ref, sc_ref, g_ref, b_ref, o_ref, buf_ref, p_ref):
    tm, halo = TM_SEQ, POOL_HALO
    start = pl.program_id(0) * tm
    length, pos0 = _seq_pos(start)
    m = mod_ref[...]
    shift, scale, gate = m[0:1], m[1:2], m[2:3]

    def modulate(v):
        return _ln(v) * (1.0 + scale) + shift

    x = x_ref[...]
    buf_ref[0:halo] = jnp.where(pos0 != 0, modulate(xp_ref[...]), 0.0)
    buf_ref[halo:halo + tm] = modulate(x)
    buf_ref[halo + tm:] = jnp.where(pos0 + tm != length, modulate(xn_ref[...]), 0.0)

    t = pos0 + lax.broadcasted_iota(jnp.int32, (tm, 1), 0)
    for g, half in enumerate(POOL_HALF):
        cols = slice(g * GROUP_W, (g + 1) * GROUP_W)
        wsum = buf_ref[halo - half:halo - half + tm, cols]
        for j in range(1 - half, half):
            wsum = wsum + buf_ref[halo + j:halo + j + tm, cols]
        cnt = (jnp.minimum(t + half, length) - jnp.maximum(t - half, 0)).astype(F32)
        pooled = wsum / cnt - buf_ref[halo:halo + tm, cols]
        p_ref[:, cols] = _dot(pooled.astype(BF16), w_ref[g])
    y = p_ref[...] * sc_ref[...]
    o_ref[...] = _post_norm(x, y, gate, g_ref[...], b_ref[...])


def _pool(x, mod, ln_g, ln_b, w_grp, scale):
    tm, halo = TM_SEQ, POOL_HALO
    nb = N_TOK // halo
    return pl.pallas_call(
        _pool_kernel,
        grid=(N_TOK // tm,),
        in_specs=[
            pl.BlockSpec((tm, D_MODEL), lambda i: (i, 0)),
            pl.BlockSpec((halo, D_MODEL), lambda i: (jnp.maximum(i * (tm // halo) - 1, 0), 0)),
            pl.BlockSpec((halo, D_MODEL), lambda i: (jnp.minimum((i + 1) * (tm // halo), nb - 1), 0)),
            _mod_spec(tm),
            _resident((N_GROUPS, GROUP_W, GROUP_W)),
            _resident((1, D_MODEL)), _resident((1, D_MODEL)), _resident((1, D_MODEL)),
        ],
        out_specs=pl.BlockSpec((tm, D_MODEL), lambda i: (i, 0)),
        out_shape=jax.ShapeDtypeStruct((N_TOK, D_MODEL), F32),
        scratch_shapes=[
            pltpu.VMEM((tm + 2 * halo, D_MODEL), F32),
            pltpu.VMEM((tm, D_MODEL), F32),
        ],
        compiler_params=_params(),
        name="pool_mixer",
    )(x, x, x, mod, w_grp.astype(BF16), scale.reshape(1, D_MODEL), ln_g.reshape(1, D_MODEL),
      ln_b.reshape(1, D_MODEL))


def _gmlp_kernel(x_ref, mod_ref, win_ref, bin_ref, vg_ref, vb_ref, ws_ref, bs_ref, wo_ref, g_ref, b_ref,
                 o_ref, act_ref):
    tm = TM_SEQ
    m = mod_ref[...]
    x = x_ref[...]
    h = (_ln(x) * (1.0 + m[1:2]) + m[0:1]).astype(BF16)
    z = jax.nn.gelu(_dot(h, win_ref[...]) + bin_ref[...])
    u = z[:, :GMLP_W]
    v = (_ln(z[:, GMLP_W:]) * vg_ref[...] + vb_ref[...]).astype(BF16)
    bs = bs_ref[...]
    for n in range(tm // CHUNK):
        rows = slice(n * CHUNK, (n + 1) * CHUNK)
        for g in range(GMLP_GROUPS):
            cols = slice(g * GMLP_GW, (g + 1) * GMLP_GW)
            s = _dot(ws_ref[g], v[rows, cols]) + bs[:, g:g + 1]
            act_ref[rows, cols] = (u[rows, cols] * s).astype(BF16)
    y = _dot(act_ref[...], wo_ref[...])
    o_ref[...] = _post_norm(x, y, m[2:3], g_ref[...], b_ref[...])


def _gmlp(x, mod, ln_g, ln_b, w_in, b_in, v_g, v_b, w_s, b_s, w_o):
    tm = TM_SEQ
    return pl.pallas_call(
        _gmlp_kernel,
        grid=(N_TOK // tm,),
        in_specs=[
            pl.BlockSpec((tm, D_MODEL), lambda i: (i, 0)),
            _mod_spec(tm),
            _resident((D_MODEL, 2 * GMLP_W)),
            _resident((1, 2 * GMLP_W)),
            _resident((1, GMLP_W)), _resident((1, GMLP_W)),
            _resident((GMLP_GROUPS, CHUNK, CHUNK)),
            _resident((CHUNK, GMLP_GROUPS)),
            _resident((GMLP_W, D_MODEL)),
            _resident((1, D_MODEL)), _resident((1, D_MODEL)),
        ],
        out_specs=pl.BlockSpec((tm, D_MODEL), lambda i: (i, 0)),
        out_shape=jax.ShapeDtypeStruct((N_TOK, D_MODEL), F32),
        scratch_shapes=[pltpu.VMEM((tm, GMLP_W), BF16)],
        compiler_params=_params(),
        name="gmlp_mixer",
    )(x, mod, w_in.astype(BF16), b_in.reshape(1, 2 * GMLP_W), v_g.reshape(1, GMLP_W),
      v_b.reshape(1, GMLP_W), w_s.astype(BF16), b_s.T, w_o.astype(BF16), ln_g.reshape(1, D_MODEL),
      ln_b.reshape(1, D_MODEL))


def kernel(x_prompt, x_sample, cache_k, cache_v, c, c_ctx, w_mod, b_mod, ln_g, ln_b, ffn_w_up, ffn_conv_w,
           ffn_conv_b, ffn_w_down, fnet_w_o, attn_w_qkv, attn_q_norm, attn_k_norm, attn_w_o, pool_w,
           pool_scale, gmlp_w_in, gmlp_b_in, gmlp_ln_g, gmlp_ln_b, gmlp_w_s, gmlp_b_s, gmlp_w_o):
    x = jnp.concatenate([x_prompt.reshape(N_P, D_MODEL), x_sample.reshape(N_S, D_MODEL)], axis=0)
    cond = jnp.concatenate([c_ctx[None, :], c, jnp.zeros((COND_PAD - N_COND, D_MODEL), F32)], axis=0)
    mod = _modulation(cond, w_mod, b_mod)

    new_k = new_v = None
    for i in range(DEPTH):
        kind, j = i % 4, i // 4
        g0, b0 = ln_g[i, 0], ln_b[i, 0]
        if kind == 0:
            x = _fnet(x, mod[i], g0, b0, fnet_w_o[j])
        elif kind == 1:
            x, new_k, new_v = _attention(x, mod[i], g0, b0, attn_w_qkv[j], attn_q_norm[j], attn_k_norm[j],
                                         attn_w_o[j], cache_k[:, j], cache_v[:, j])
        elif kind == 2:
            x = _pool(x, mod[i], g0, b0, pool_w[j], pool_scale[j])
        else:
            x = _gmlp(x, mod[i], g0, b0, gmlp_w_in[j], gmlp_b_in[j], gmlp_ln_g[j], gmlp_ln_b[j],
                      gmlp_w_s[j], gmlp_b_s[j], gmlp_w_o[j])
        x = _ffn(x, mod[i], ln_g[i, 1], ln_b[i, 1], ffn_w_up[i], ffn_conv_w[i], ffn_conv_b[i], ffn_w_down[i])

    y_prompt = x[:N_P].reshape(BATCH, SEQ, D_MODEL)
    y_sample = x[N_P:].reshape(DEC_BATCH, DEC_SEQ, D_MODEL)
    new_k = new_k.reshape(BATCH, 1, SEQ, N_KV_HEADS, HEAD_DIM)
    new_v = new_v.reshape(BATCH, 1, SEQ, N_KV_HEADS, HEAD_DIM)
    return (y_prompt, y_sample, new_k, new_v)
```

```python
import functools
import math

import numpy as np
import jax
import jax.numpy as jnp
from jax import lax
from jax.experimental import pallas as pl
from jax.experimental.pallas import tpu as pltpu

D_MODEL = 1024
BATCH = 32
SEQ = 256
DEPTH = 4
DEC_BATCH = 2
DEC_SEQ = 4096
PAST_LEN = 512
GRID_W = 64
N_GROUPS = 4
GROUP_W = D_MODEL // N_GROUPS
HEAD_DIM = 128
N_Q_HEADS = D_MODEL // HEAD_DIM
N_KV_HEADS = 2
Q_PER_KV = N_Q_HEADS // N_KV_HEADS
Q_W = N_Q_HEADS * HEAD_DIM
KV_W = N_KV_HEADS * HEAD_DIM
QKV_W = Q_W + 2 * KV_W
ROPE_THETA = 10000.0
POOL_HALF = (1, 2, 4, 8)
CHUNK = 128
GMLP_W = 2 * D_MODEL
GMLP_GROUPS = 4
GMLP_GW = GMLP_W // GMLP_GROUPS
D_FF = ((8 * D_MODEL // 3 + 127) // 128) * 128
ALPHA = (2 * DEPTH) ** 0.25
LN_EPS = 1e-6

N_P = BATCH * SEQ
N_S = DEC_BATCH * DEC_SEQ
N_TOK = N_P + N_S
N_COND = 1 + DEC_BATCH
COND_PAD = 8

F32 = jnp.float32
BF16 = jnp.bfloat16

VMEM_LIMIT_BYTES = 56 * 1024 * 1024
BF16_ROWS = 16

TM_SEQ = SEQ
TM_FFN = 512
TM_QKV = 512
TM_GMLP = 512
FFN_CHUNK = D_FF // 2
DFT_LANES = 128
N_TA = DEC_SEQ // DFT_LANES
ATTN_PAIR = 2


def _dot(a, b):
    return jnp.dot(a, b, preferred_element_type=F32)


def _dot_nt(a, b):
    return lax.dot_general(a, b, (((1,), (1,)), ((), ())), preferred_element_type=F32)


def _ln(x):
    mu = jnp.mean(x, axis=-1, keepdims=True)
    xc = x - mu
    var = jnp.mean(xc * xc, axis=-1, keepdims=True)
    return xc * lax.rsqrt(var + LN_EPS)


def _post_norm(x, y, gate, g, b):
    return _ln(ALPHA * x + gate * y) * g + b


def _cond_row(row_start):
    return jnp.maximum(row_start - (N_P - DEC_SEQ), 0) // DEC_SEQ


def _seq_pos(row_start):
    is_p = row_start < N_P
    length = jnp.where(is_p, SEQ, DEC_SEQ)
    pos = jnp.where(is_p, row_start, row_start - N_P) & (length - 1)
    return length, pos


def _params():
    return pltpu.CompilerParams(dimension_semantics=("arbitrary",), vmem_limit_bytes=VMEM_LIMIT_BYTES)


def _resident(shape):
    return pl.BlockSpec(shape, lambda i: (0,) * len(shape), pipeline_mode=pl.Buffered(1))


def _layer(tail, layer):
    return pl.BlockSpec((None,) + tuple(tail), lambda i: (layer,) + (0,) * len(tail),
                        pipeline_mode=pl.Buffered(1))


def _mod_spec(tm, layer):
    return pl.BlockSpec((None, None, 6, D_MODEL), lambda i: (layer, _cond_row(i * tm), 0, 0))


def _ln_spec(layer, which):
    return pl.BlockSpec((None, 1, D_MODEL), lambda i: (2 * layer + which, 0, 0),
                        pipeline_mode=pl.Buffered(1))


def _row_spec(tm, width=D_MODEL):
    return pl.BlockSpec((tm, width), lambda i: (i, 0))


def _stream_specs(tm, width=D_MODEL):
    n_pt = N_P // tm
    return (pl.BlockSpec((tm, width), lambda i: (jnp.minimum(i, n_pt - 1), 0)),
            pl.BlockSpec((tm, width), lambda i: (jnp.maximum(i - n_pt, 0), 0)))


def _mod_kernel(c_ref, w_ref, b_ref, o_ref):
    c = c_ref[...]
    s = c / (1.0 + jnp.exp(-c))
    o_ref[...] = _dot(s.astype(BF16), w_ref[...].astype(BF16)) + b_ref[...]


def _modulation(cond, w_mod, b_mod):
    tn = 2048
    out = pl.pallas_call(
        _mod_kernel,
        grid=(DEPTH, 6 * D_MODEL // tn),
        in_specs=[
            pl.BlockSpec((COND_PAD, D_MODEL), lambda l, j: (0, 0)),
            pl.BlockSpec((None, D_MODEL, tn), lambda l, j: (l, 0, j)),
            pl.BlockSpec((None, 1, tn), lambda l, j: (l, 0, j)),
        ],
        out_specs=pl.BlockSpec((None, COND_PAD, tn), lambda l, j: (l, 0, j)),
        out_shape=jax.ShapeDtypeStruct((DEPTH, COND_PAD, 6 * D_MODEL), F32),
        compiler_params=pltpu.CompilerParams(
            dimension_semantics=("arbitrary", "arbitrary"), vmem_limit_bytes=VMEM_LIMIT_BYTES),
        name="modulation",
    )(cond, w_mod, b_mod.reshape(DEPTH, 1, 6 * D_MODEL))
    return out[:, :N_COND].reshape(DEPTH, N_COND, 6, D_MODEL)


def _ffn_kernel(x_ref, xp_ref, xn_ref, mod_ref, g_ref, b_ref, wup_ref, cw_ref, cb_ref, wdn_ref,
                *rest, tm, split_out):
    halo = BF16_ROWS
    out_refs, (hext_ref, aext_ref) = rest[:-2], rest[-2:]
    start = pl.program_id(0) * tm
    length, pos0 = _seq_pos(start)
    m = mod_ref[...]
    shift, scale, gate = m[3:4], m[4:5], m[5:6]

    def modulate(v):
        return (_ln(v) * (1.0 + scale) + shift).astype(BF16)

    x = x_ref[...]
    hext_ref[0:halo] = modulate(xp_ref[...])
    hext_ref[halo:halo + tm] = modulate(x)
    hext_ref[halo + tm:] = modulate(xn_ref[...])

    t = (pos0 + lax.broadcasted_iota(jnp.int32, (tm, 1), 0)) & (length - 1)
    has_prev = t != 0
    has_next = t != length - 1

    acc = jnp.zeros((tm, D_MODEL), F32)
    for c in range(D_FF // FFN_CHUNK):
        lo = c * FFN_CHUNK
        aext_ref[...] = _dot(hext_ref[...], wup_ref[:, lo:lo + FFN_CHUNK])
        bgate = _dot(hext_ref[halo:halo + tm], wup_ref[:, D_FF + lo:D_FF + lo + FFN_CHUNK])
        cw = cw_ref[:, lo:lo + FFN_CHUNK]
        a = (jnp.where(has_prev, aext_ref[halo - 1:halo - 1 + tm], 0.0) * cw[0:1]
             + aext_ref[halo:halo + tm] * cw[1:2]
             + jnp.where(has_next, aext_ref[halo + 1:halo + 1 + tm], 0.0) * cw[2:3]
             + cb_ref[:, lo:lo + FFN_CHUNK])
        act = (jax.nn.gelu(a) * bgate).astype(BF16)
        acc = acc + _dot(act, wdn_ref[lo:lo + FFN_CHUNK, :])
    y = _post_norm(x, acc, gate, g_ref[...], b_ref[...])

    if split_out:
        @pl.when(start < N_P)
        def _():
            out_refs[0][...] = y

        @pl.when(start >= N_P)
        def _():
            out_refs[1][...] = y
    else:
        out_refs[0][...] = y


def _ffn(x, mod, ln_g, ln_b, w_up, conv_w, conv_b, w_down, layer, split_out):
    tm, halo = TM_FFN, BF16_ROWS
    nb = N_TOK // halo
    if split_out:
        out_specs = list(_stream_specs(tm))
        out_shape = [jax.ShapeDtypeStruct((N_P, D_MODEL), F32), jax.ShapeDtypeStruct((N_S, D_MODEL), F32)]
    else:
        out_specs = _row_spec(tm)
        out_shape = jax.ShapeDtypeStruct((N_TOK, D_MODEL), F32)
    return pl.pallas_call(
        functools.partial(_ffn_kernel, tm=tm, split_out=split_out),
        grid=(N_TOK // tm,),
        in_specs=[
            _row_spec(tm),
            pl.BlockSpec((halo, D_MODEL), lambda i: (jnp.maximum(i * (tm // halo) - 1, 0), 0)),
            pl.BlockSpec((halo, D_MODEL), lambda i: (jnp.minimum((i + 1) * (tm // halo), nb - 1), 0)),
            _mod_spec(tm, layer),
            _ln_spec(layer, 1), _ln_spec(layer, 1),
            _layer((D_MODEL, 2 * D_FF), layer),
            _layer((3, D_FF), layer),
            _layer((1, D_FF), layer),
            _layer((D_FF, D_MODEL), layer),
        ],
        out_specs=out_specs,
        out_shape=out_shape,
        scratch_shapes=[
            pltpu.VMEM((tm + 2 * halo, D_MODEL), BF16),
            pltpu.VMEM((tm + 2 * halo, FFN_CHUNK), F32),
        ],
        compiler_params=_params(),
        name="conv_ffn",
    )(x, x, x, mod, ln_g, ln_b, w_up, conv_w, conv_b, w_down)


def _dft_tables():
    k = np.arange(SEQ, dtype=np.int64)
    ang = 2.0 * np.pi * ((k[:, None] * k[None, :]) % SEQ) / SEQ
    c256, s256 = np.cos(ang), np.sin(ang)
    k = np.arange(DEC_SEQ, dtype=np.int64)
    tb = np.arange(DFT_LANES, dtype=np.int64)
    ang_b = 2.0 * np.pi * ((k[:, None] * tb[None, :]) % DEC_SEQ) / DEC_SEQ
    ta = np.arange(N_TA, dtype=np.int64)
    ang_a = 2.0 * np.pi * ((k[:, None] * ta[None, :]) % N_TA) / N_TA
    pad = np.zeros((DEC_SEQ, DFT_LANES - N_TA))
    f = lambda a: np.asarray(a, np.float32)
    return (f(c256), f(s256), f(np.cos(ang_b)), f(np.sin(ang_b)),
            f(np.concatenate([np.cos(ang_a), pad], 1)), f(np.concatenate([np.sin(ang_a), pad], 1)))


def _fnet_a_kernel(xp_ref, xs_ref, mod_ref, cs_ref, p_ref, q_ref, *, tm):
    x = jnp.where(pl.program_id(0) < N_P // tm, xp_ref[...], xs_ref[...])
    m = mod_ref[...]
    h = (_ln(x) * (1.0 + m[1:2]) + m[0:1]).astype(BF16)
    for g in range(N_GROUPS):
        r = _dot(h[:, g * GROUP_W:(g + 1) * GROUP_W], cs_ref[...])
        p_ref[:, g * GROUP_W:(g + 1) * GROUP_W] = r[:, :GROUP_W].astype(BF16)
        q_ref[:, g * GROUP_W:(g + 1) * GROUP_W] = r[:, GROUP_W:].astype(BF16)


def _fnet_b_kernel(xp_ref, xs_ref, ps_ref, qs_ref, pb_ref, qb_ref, c256_ref, s256_ref, cb_ref, sb_ref,
                   ca_ref, sa_ref, wo_ref, mod_ref, g_ref, b_ref, o_ref, ct_ref, st_ref, f_ref):
    i = pl.program_id(0)
    is_prompt = i < N_P // TM_SEQ

    @pl.when(is_prompt)
    def _():
        f = _dot(c256_ref[...], ps_ref[...]) - _dot(s256_ref[...], qs_ref[...])
        f_ref[...] = (f * (SEQ * GROUP_W) ** -0.5).astype(BF16)

    @pl.when(jnp.logical_not(is_prompt))
    def _():
        cb, sb = cb_ref[...], sb_ref[...]
        ca, sa = ca_ref[...], sa_ref[...]
        for ta in range(N_TA):
            cca, ssa = ca[:, ta:ta + 1], sa[:, ta:ta + 1]
            ct_ref[:, ta * DFT_LANES:(ta + 1) * DFT_LANES] = (cca * cb - ssa * sb).astype(BF16)
            st_ref[:, ta * DFT_LANES:(ta + 1) * DFT_LANES] = (ssa * cb + cca * sb).astype(BF16)
        f = _dot(ct_ref[...], pb_ref[...]) - _dot(st_ref[...], qb_ref[...])
        f_ref[...] = (f * (DEC_SEQ * GROUP_W) ** -0.5).astype(BF16)

    x = jnp.where(is_prompt, xp_ref[...], xs_ref[...])
    y = _dot(f_ref[...], wo_ref[...])
    o_ref[...] = _post_norm(x, y, mod_ref[...][2:3], g_ref[...], b_ref[...])


def _fnet(xp, xs, mod, ln_g, ln_b, w_o, layer, j):
    c256, s256, cb, sb, ca, sa = _dft_tables()
    cs = jnp.concatenate([jnp.asarray(c256), jnp.asarray(s256)], axis=1).astype(BF16)
    tm = 512
    p, q = pl.pallas_call(
        functools.partial(_fnet_a_kernel, tm=tm),
        grid=(N_TOK // tm,),
        in_specs=[*_stream_specs(tm), _mod_spec(tm, layer), _resident((GROUP_W, 2 * GROUP_W))],
        out_specs=[_row_spec(tm)] * 2,
        out_shape=[jax.ShapeDtypeStruct((N_TOK, D_MODEL), BF16)] * 2,
        compiler_params=_params(),
        name="fnet_channel_dft",
    )(xp, xs, mod, cs)

    tm = TM_SEQ
    n_pt = N_P // tm
    per_seq = DEC_SEQ // tm
    small = pl.BlockSpec((tm, D_MODEL), lambda i: (jnp.minimum(i, n_pt - 1), 0))
    big = pl.BlockSpec((DEC_SEQ, D_MODEL),
                       lambda i: (N_P // DEC_SEQ + jnp.maximum(i - n_pt, 0) // per_seq, 0),
                       pipeline_mode=pl.Buffered(1))
    tab = pl.BlockSpec((tm, DFT_LANES), lambda i: (jnp.maximum(i - n_pt, 0) % per_seq, 0))
    return pl.pallas_call(
        _fnet_b_kernel,
        grid=(N_TOK // tm,),
        in_specs=[
            *_stream_specs(tm),
            small, small, big, big,
            _resident((SEQ, SEQ)), _resident((SEQ, SEQ)),
            tab, tab, tab, tab,
            _layer((D_MODEL, D_MODEL), j),
            _mod_spec(tm, layer),
            _ln_spec(layer, 0), _ln_spec(layer, 0),
        ],
        out_specs=_row_spec(tm),
        out_shape=jax.ShapeDtypeStruct((N_TOK, D_MODEL), F32),
        scratch_shapes=[
            pltpu.VMEM((tm, DEC_SEQ), BF16),
            pltpu.VMEM((tm, DEC_SEQ), BF16),
            pltpu.VMEM((tm, D_MODEL), BF16),
        ],
        compiler_params=_params(),
        name="fnet_token_dft",
    )(xp, xs, p, q, p, q, jnp.asarray(c256).astype(BF16), jnp.asarray(s256).astype(BF16),
      jnp.asarray(cb), jnp.asarray(sb), jnp.asarray(ca), jnp.asarray(sa), w_o, mod, ln_g, ln_b)


def _rope_tables(tm):
    quarter = HEAD_DIM // 4
    t = np.arange(DEC_SEQ)
    row, col = (t // GRID_W).astype(np.float32), (t % GRID_W).astype(np.float32)
    inv = (np.float32(ROPE_THETA) ** (-np.arange(quarter, dtype=np.float32) / np.float32(quarter))).astype(np.float32)
    ang_r = (row[:, None] * inv[None, :]).astype(np.float32).astype(np.float64)
    ang_c = (col[:, None] * inv[None, :]).astype(np.float32).astype(np.float64)
    cos = np.concatenate([np.cos(ang_r), np.cos(ang_r), np.cos(ang_c), np.cos(ang_c)], axis=1)
    sin = np.concatenate([-np.sin(ang_r), np.sin(ang_r), -np.sin(ang_c), np.sin(ang_c)], axis=1)
    cos = np.concatenate([cos, np.ones((tm, HEAD_DIM))], axis=0)
    sin = np.concatenate([sin, np.zeros((tm, HEAD_DIM))], axis=0)
    return np.asarray(cos, np.float32), np.asarray(sin, np.float32)


def _qkv_kernel(x_ref, mod_ref, w_ref, qg_ref, kg_ref, cos_ref, sin_ref,
                q_ref, k_ref, v_ref, vt_ref, nk_ref, nv_ref, *, tm):
    i = pl.program_id(0)
    m = mod_ref[...]
    h = (_ln(x_ref[...]) * (1.0 + m[1:2]) + m[0:1]).astype(BF16)
    qkv = _dot(h, w_ref[...])
    cos, sin = cos_ref[...], sin_ref[...]
    lane = lax.broadcasted_iota(jnp.int32, (1, HEAD_DIM), 1)
    low = (lane % (HEAD_DIM // 2)) < (HEAD_DIM // 4)

    def rms(v, g):
        return v * lax.rsqrt(jnp.mean(v * v, axis=-1, keepdims=True) + LN_EPS) * g

    def rope(v):
        partner = jnp.where(low, pltpu.roll(v, HEAD_DIM - HEAD_DIM // 4, 1), pltpu.roll(v, HEAD_DIM // 4, 1))
        return v * cos + partner * sin

    is_prompt = i < N_P // tm
    for hq in range(N_Q_HEADS):
        sl = slice(hq * HEAD_DIM, (hq + 1) * HEAD_DIM)
        q_ref[:, sl] = rope(rms(qkv[:, sl], qg_ref[...])).astype(BF16)
    for hk in range(N_KV_HEADS):
        sl = slice(hk * HEAD_DIM, (hk + 1) * HEAD_DIM)
        kn = rms(qkv[:, Q_W + hk * HEAD_DIM:Q_W + (hk + 1) * HEAD_DIM], kg_ref[...])
        k_ref[:, sl] = rope(kn).astype(BF16)

        @pl.when(is_prompt)
        def _():
            nk_ref[:, sl] = kn

    v = qkv[:, Q_W + KV_W:]
    v_ref[...] = v.astype(BF16)

    @pl.when(is_prompt)
    def _():
        nv_ref[...] = v

    @pl.when(jnp.logical_not(is_prompt))
    def _():
        vt_ref[...] = v.T.astype(BF16)


def _attn_kernel(x_ref, q_ref, ks_ref, vs_ref, kb_ref, vbt_ref, kc_ref, vct_ref, wo_ref, mod_ref, g_ref,
                 b_ref, o_ref, att_ref):
    i = pl.program_id(0)
    tm = TM_SEQ
    is_prompt = i < N_P // tm

    def head(hq):
        return slice(hq * HEAD_DIM, (hq + 1) * HEAD_DIM)

    @pl.when(is_prompt)
    def _():
        for hk in range(N_KV_HEADS):
            q4 = jnp.concatenate([q_ref[:, head(hk * Q_PER_KV + g)] for g in range(Q_PER_KV)], axis=0)
            s = _dot_nt(q4, ks_ref[:, head(hk)])
            p = jnp.exp2(s - jnp.max(s, axis=-1, keepdims=True))
            o4 = _dot(p.astype(BF16), vs_ref[:, head(hk)]) / jnp.sum(p, axis=-1, keepdims=True)
            for g in range(Q_PER_KV):
                att_ref[:, head(hk * Q_PER_KV + g)] = o4[g * tm:(g + 1) * tm].astype(BF16)

    @pl.when(jnp.logical_not(is_prompt))
    def _():
        for hk in range(N_KV_HEADS):
            for g0 in range(0, Q_PER_KV, ATTN_PAIR):
                hq0 = hk * Q_PER_KV + g0
                qq = jnp.concatenate([q_ref[:, head(hq0 + g)] for g in range(ATTN_PAIR)], axis=0)
                st_b = _dot_nt(kb_ref[:, head(hk)], qq)
                st_c = _dot_nt(kc_ref[:, head(hk)], qq)
                mx = jnp.maximum(jnp.max(st_b, axis=0, keepdims=True), jnp.max(st_c, axis=0, keepdims=True))
                pt_b = jnp.exp2(st_b - mx)
                pt_c = jnp.exp2(st_c - mx)
                den = jnp.sum(pt_b, axis=0, keepdims=True) + jnp.sum(pt_c, axis=0, keepdims=True)
                ot = (_dot(vbt_ref[head(hk), :], pt_b.astype(BF16))
                      + _dot(vct_ref[head(hk), :], pt_c.astype(BF16))) / den
                oo = ot.T
                for g in range(ATTN_PAIR):
                    att_ref[:, head(hq0 + g)] = oo[g * tm:(g + 1) * tm].astype(BF16)

    y = _dot(att_ref[...], wo_ref[...])
    o_ref[...] = _post_norm(x_ref[...], y, mod_ref[...][2:3], g_ref[...], b_ref[...])


def _attention(x, mod, ln_g, ln_b, w_qkv, q_norm, k_norm, w_o, cache_k, cache_v, layer, j):
    tm = TM_QKV
    n_pt = N_P // tm
    cos, sin = _rope_tables(tm)
    rope_spec = pl.BlockSpec(
        (tm, HEAD_DIM),
        lambda i: (jnp.where(i < n_pt, DEC_SEQ // tm, jnp.maximum(i - n_pt, 0) % (DEC_SEQ // tm)), 0))
    prompt_row = _stream_specs(tm, KV_W)[0]
    latent_col = pl.BlockSpec((KV_W, tm), lambda i: (0, jnp.maximum(i - n_pt, 0)))
    q_gain = q_norm[j] * (math.log2(math.e) * HEAD_DIM ** -0.5)
    q, k, v, vt, new_k, new_v = pl.pallas_call(
        functools.partial(_qkv_kernel, tm=tm),
        grid=(N_TOK // tm,),
        in_specs=[
            _row_spec(tm),
            _mod_spec(tm, layer),
            _layer((D_MODEL, QKV_W), j),
            _resident((1, HEAD_DIM)), _resident((1, HEAD_DIM)),
            rope_spec, rope_spec,
        ],
        out_specs=[_row_spec(tm, Q_W), _row_spec(tm, KV_W), _row_spec(tm, KV_W), latent_col,
                   prompt_row, prompt_row],
        out_shape=[
            jax.ShapeDtypeStruct((N_TOK, Q_W), BF16),
            jax.ShapeDtypeStruct((N_TOK, KV_W), BF16),
            jax.ShapeDtypeStruct((N_TOK, KV_W), BF16),
            jax.ShapeDtypeStruct((KV_W, N_S), BF16),
            jax.ShapeDtypeStruct((N_P, KV_W), F32),
            jax.ShapeDtypeStruct((N_P, KV_W), F32),
        ],
        compiler_params=_params(),
        name="qkv_rope",
    )(x, mod, w_qkv, q_gain.reshape(1, HEAD_DIM), k_norm[j].reshape(1, HEAD_DIM), jnp.asarray(cos),
      jnp.asarray(sin))

    tm = TM_SEQ
    n_pt = N_P // tm
    per_seq = DEC_SEQ // tm
    batch_of = lambda i: jnp.maximum(i - n_pt, 0) // per_seq
    small = _stream_specs(tm, KV_W)[0]
    k_big = pl.BlockSpec((DEC_SEQ, KV_W), lambda i: (N_P // DEC_SEQ + batch_of(i), 0))
    vt_big = pl.BlockSpec((KV_W, DEC_SEQ), lambda i: (0, batch_of(i)))
    kc_spec = pl.BlockSpec((None, PAST_LEN, KV_W), lambda i: (batch_of(i), 0, 0))
    vct_spec = pl.BlockSpec((None, KV_W, PAST_LEN), lambda i: (batch_of(i), 0, 0))
    kc = cache_k[:, j].reshape(DEC_BATCH, PAST_LEN, KV_W).astype(BF16)
    vct = cache_v[:, j].reshape(DEC_BATCH, PAST_LEN, KV_W).transpose(0, 2, 1).astype(BF16)
    x_new = pl.pallas_call(
        _attn_kernel,
        grid=(N_TOK // tm,),
        in_specs=[
            _row_spec(tm), _row_spec(tm, Q_W),
            small, small, k_big, vt_big, kc_spec, vct_spec,
            _layer((Q_W, D_MODEL), j),
            _mod_spec(tm, layer),
            _ln_spec(layer, 0), _ln_spec(layer, 0),
        ],
        out_specs=_row_spec(tm),
        out_shape=jax.ShapeDtypeStruct((N_TOK, D_MODEL), F32),
        scratch_shapes=[pltpu.VMEM((tm, Q_W), BF16)],
        compiler_params=_params(),
        name="attention",
    )(x, q, k, v, k, vt, kc, vct, w_o, mod, ln_g, ln_b)
    return x_new, new_k, new_v


POOL_HALO = 8


def _pool_kernel(x_ref, xp_ref, xn_ref, mod_ref, w_ref, sc_ref, g_ref, b_ref, o_ref, buf_ref, p_ref):
    tm, halo = TM_SEQ, POOL_HALO
    start = pl.program_id(0) * tm
    length, pos0 = _seq_pos(start)
    m = mod_ref[...]
    shift, scale, gate = m[0:1], m[1:2], m[2:3]

    def modulate(v):
        return _ln(v) * (1.0 + scale) + shift

    x = x_ref[...]
    buf_ref[0:halo] = jnp.where(pos0 != 0, modulate(xp_ref[...]), 0.0)
    buf_ref[halo:halo + tm] = modulate(x)
    buf_ref[halo + tm:] = jnp.where(pos0 + tm != length, modulate(xn_ref[...]), 0.0)

    t = pos0 + lax.broadcasted_iota(jnp.int32, (tm, 1), 0)
    for g, half in enumerate(POOL_HALF):
        cols = slice(g * GROUP_W, (g + 1) * GROUP_W)
        wsum = buf_ref[halo - half:halo - half + tm, cols]
        for j in range(1 - half, half):
            wsum = wsum + buf_ref[halo + j:halo + j + tm, cols]
        cnt = (jnp.minimum(t + half, length) - jnp.maximum(t - half, 0)).astype(F32)
        pooled = wsum / cnt - buf_ref[halo:halo + tm, cols]
        p_ref[:, cols] = _dot(pooled.astype(BF16), w_ref[g])
    y = p_ref[...] * sc_ref[...]
    o_ref[...] = _post_norm(x, y, gate, g_ref[...], b_ref[...])


def _pool(x, mod, ln_g, ln_b, w_grp, scale, layer, j):
    tm, halo = TM_SEQ, POOL_HALO
    nb = N_TOK // halo
    return pl.pallas_call(
        _pool_kernel,
        grid=(N_TOK // tm,),
        in_specs=[
            _row_spec(tm),
            pl.BlockSpec((halo, D_MODEL), lambda i: (jnp.maximum(i * (tm // halo) - 1, 0), 0)),
            pl.BlockSpec((halo, D_MODEL), lambda i: (jnp.minimum((i + 1) * (tm // halo), nb - 1), 0)),
            _mod_spec(tm, layer),
            _layer((N_GROUPS, GROUP_W, GROUP_W), j),
            _layer((1, D_MODEL), j),
            _ln_spec(layer, 0), _ln_spec(layer, 0),
        ],
        out_specs=_row_spec(tm),
        out_shape=jax.ShapeDtypeStruct((N_TOK, D_MODEL), F32),
        scratch_shapes=[
            pltpu.VMEM((tm + 2 * halo, D_MODEL), F32),
            pltpu.VMEM((tm, D_MODEL), F32),
        ],
        compiler_params=_params(),
        name="pool_mixer",
    )(x, x, x, mod, w_grp, scale, ln_g, ln_b)


def _gmlp_kernel(x_ref, mod_ref, win_ref, bin_ref, vg_ref, vb_ref, ws_ref, bs_ref, wo_ref, g_ref, b_ref,
                 o_ref, act_ref, *, tm):
    m = mod_ref[...]
    x = x_ref[...]
    h = (_ln(x) * (1.0 + m[1:2]) + m[0:1]).astype(BF16)
    z = jax.nn.gelu(_dot(h, win_ref[...]) + bin_ref[...])
    u = z[:, :GMLP_W]
    v = (_ln(z[:, GMLP_W:]) * vg_ref[...] + vb_ref[...]).astype(BF16)
    bs = bs_ref[...]
    for n in range(tm // CHUNK):
        rows = slice(n * CHUNK, (n + 1) * CHUNK)
        for g in range(GMLP_GROUPS):
            cols = slice(g * GMLP_GW, (g + 1) * GMLP_GW)
            s = _dot(ws_ref[g], v[rows, cols]) + bs[:, g:g + 1]
            act_ref[rows, cols] = (u[rows, cols] * s).astype(BF16)
    y = _dot(act_ref[...], wo_ref[...])
    o_ref[...] = _post_norm(x, y, m[2:3], g_ref[...], b_ref[...])


def _gmlp(x, mod, ln_g, ln_b, w_in, b_in, v_g, v_b, w_s, b_s_t, w_o, layer, j):
    tm = TM_GMLP
    return pl.pallas_call(
        functools.partial(_gmlp_kernel, tm=tm),
        grid=(N_TOK // tm,),
        in_specs=[
            _row_spec(tm),
            _mod_spec(tm, layer),
            _layer((D_MODEL, 2 * GMLP_W), j),
            _layer((1, 2 * GMLP_W), j),
            _layer((1, GMLP_W), j), _layer((1, GMLP_W), j),
            _layer((GMLP_GROUPS, CHUNK, CHUNK), j),
            _layer((CHUNK, GMLP_GROUPS), j),
            _layer((GMLP_W, D_MODEL), j),
            _ln_spec(layer, 0), _ln_spec(layer, 0),
        ],
        out_specs=_row_spec(tm),
        out_shape=jax.ShapeDtypeStruct((N_TOK, D_MODEL), F32),
        scratch_shapes=[pltpu.VMEM((tm, GMLP_W), BF16)],
        compiler_params=_params(),
        name="gmlp_mixer",
    )(x, mod, w_in, b_in, v_g, v_b, w_s, b_s_t, w_o, ln_g, ln_b)


def kernel(x_prompt, x_sample, cache_k, cache_v, c, c_ctx, w_mod, b_mod, ln_g, ln_b, ffn_w_up, ffn_conv_w,
           ffn_conv_b, ffn_w_down, fnet_w_o, attn_w_qkv, attn_q_norm, attn_k_norm, attn_w_o, pool_w,
           pool_scale, gmlp_w_in, gmlp_b_in, gmlp_ln_g, gmlp_ln_b, gmlp_w_s, gmlp_b_s, gmlp_w_o):
    assert DEPTH == 4
    cond = jnp.concatenate([c_ctx[None, :], c, jnp.zeros((COND_PAD - N_COND, D_MODEL), F32)], axis=0)
    mod = _modulation(cond, w_mod, b_mod)
    ln_g = ln_g.reshape(DEPTH * 2, 1, D_MODEL)
    ln_b = ln_b.reshape(DEPTH * 2, 1, D_MODEL)
    ffn = (ffn_w_up.astype(BF16), ffn_conv_w, ffn_conv_b.reshape(DEPTH, 1, D_FF), ffn_w_down.astype(BF16))

    x = _fnet(x_prompt.reshape(N_P, D_MODEL), x_sample.reshape(N_S, D_MODEL), mod, ln_g, ln_b,
              fnet_w_o.astype(BF16), 0, 0)
    x = _ffn(x, mod, ln_g, ln_b, *ffn, 0, False)
    x, new_k, new_v = _attention(x, mod, ln_g, ln_b, attn_w_qkv.astype(BF16), attn_q_norm, attn_k_norm,
                                 attn_w_o.astype(BF16), cache_k, cache_v, 1, 0)
    x = _ffn(x, mod, ln_g, ln_b, *ffn, 1, False)
    x = _pool(x, mod, ln_g, ln_b, pool_w.astype(BF16), pool_scale.reshape(-1, 1, D_MODEL), 2, 0)
    x = _ffn(x, mod, ln_g, ln_b, *ffn, 2, False)
    x = _gmlp(x, mod, ln_g, ln_b, gmlp_w_in.astype(BF16), gmlp_b_in.reshape(-1, 1, 2 * GMLP_W),
              gmlp_ln_g.reshape(-1, 1, GMLP_W), gmlp_ln_b.reshape(-1, 1, GMLP_W), gmlp_w_s.astype(BF16),
              gmlp_b_s.transpose(0, 2, 1), gmlp_w_o.astype(BF16), 3, 0)
    y_prompt, y_sample = _ffn(x, mod, ln_g, ln_b, *ffn, 3, True)

    return (y_prompt.reshape(BATCH, SEQ, D_MODEL), y_sample.reshape(DEC_BATCH, DEC_SEQ, D_MODEL),
            new_k.reshape(BATCH, 1, SEQ, N_KV_HEADS, HEAD_DIM), new_v.reshape(BATCH, 1, SEQ, N_KV_HEADS, HEAD_DIM))
```

```python
import functools
import math

import numpy as np
import jax
import jax.numpy as jnp
from jax import lax
from jax.experimental import pallas as pl
from jax.experimental.pallas import tpu as pltpu

D_MODEL = 1024
BATCH = 32
SEQ = 256
DEPTH = 4
DEC_BATCH = 2
DEC_SEQ = 4096
PAST_LEN = 512
GRID_W = 64
N_GROUPS = 4
GROUP_W = D_MODEL // N_GROUPS
HEAD_DIM = 128
N_Q_HEADS = D_MODEL // HEAD_DIM
N_KV_HEADS = 2
Q_PER_KV = N_Q_HEADS // N_KV_HEADS
Q_W = N_Q_HEADS * HEAD_DIM
KV_W = N_KV_HEADS * HEAD_DIM
QKV_W = Q_W + 2 * KV_W
ROPE_THETA = 10000.0
POOL_HALF = (1, 2, 4, 8)
CHUNK = 128
GMLP_W = 2 * D_MODEL
GMLP_GROUPS = 4
GMLP_GW = GMLP_W // GMLP_GROUPS
D_FF = ((8 * D_MODEL // 3 + 127) // 128) * 128
ALPHA = (2 * DEPTH) ** 0.25
LN_EPS = 1e-6

N_P = BATCH * SEQ
N_S = DEC_BATCH * DEC_SEQ
N_TOK = N_P + N_S
N_COND = 1 + DEC_BATCH
COND_PAD = 8

F32 = jnp.float32
BF16 = jnp.bfloat16

VMEM_LIMIT_BYTES = 56 * 1024 * 1024
BF16_ROWS = 16

TM_SEQ = SEQ
TM_FFN = 512
TM_QKV = 512
TM_GMLP = 512
LANES = 128
MXU_DIM = 256
FFN_CHUNKS = (5 * MXU_DIM, 6 * MXU_DIM)
assert sum(FFN_CHUNKS) == D_FF
DFT_RADIX = 8
DFT_SUB = DEC_SEQ // DFT_RADIX
DFT_COEF_ROWS = 128
DFT_CH = 512
ATTN_PAIR = 2


def _dot(a, b):
    return jnp.dot(a, b, preferred_element_type=F32)


def _dot_nt(a, b):
    return lax.dot_general(a, b, (((1,), (1,)), ((), ())), preferred_element_type=F32)


def _ln(x):
    mu = jnp.mean(x, axis=-1, keepdims=True)
    xc = x - mu
    var = jnp.mean(xc * xc, axis=-1, keepdims=True)
    return xc * lax.rsqrt(var + LN_EPS)


def _post_norm(x, y, gate, g, b):
    return _ln(ALPHA * x + gate * y) * g + b


def _cond_row(row_start):
    return jnp.maximum(row_start - (N_P - DEC_SEQ), 0) // DEC_SEQ


def _seq_pos(row_start):
    is_p = row_start < N_P
    length = jnp.where(is_p, SEQ, DEC_SEQ)
    pos = jnp.where(is_p, row_start, row_start - N_P) & (length - 1)
    return length, pos


def _params():
    return pltpu.CompilerParams(dimension_semantics=("arbitrary",), vmem_limit_bytes=VMEM_LIMIT_BYTES)


def _resident(shape):
    return pl.BlockSpec(shape, lambda i: (0,) * len(shape), pipeline_mode=pl.Buffered(1))


def _layer(tail, layer):
    return pl.BlockSpec((None,) + tuple(tail), lambda i: (layer,) + (0,) * len(tail),
                        pipeline_mode=pl.Buffered(1))


def _mod_spec(tm, layer):
    return pl.BlockSpec((None, None, 6, D_MODEL), lambda i: (layer, _cond_row(i * tm), 0, 0))


def _ln_spec(layer, which):
    return pl.BlockSpec((None, 1, D_MODEL), lambda i: (2 * layer + which, 0, 0),
                        pipeline_mode=pl.Buffered(1))


def _row_spec(tm, width=D_MODEL):
    return pl.BlockSpec((tm, width), lambda i: (i, 0))


def _stream_specs(tm, width=D_MODEL):
    n_pt = N_P // tm
    return (pl.BlockSpec((tm, width), lambda i: (jnp.minimum(i, n_pt - 1), 0)),
            pl.BlockSpec((tm, width), lambda i: (jnp.maximum(i - n_pt, 0), 0)))


def _mod_kernel(c_ref, w_ref, b_ref, o_ref):
    c = c_ref[...]
    s = c / (1.0 + jnp.exp(-c))
    o_ref[...] = _dot(s.astype(BF16), w_ref[...].astype(BF16)) + b_ref[...]


def _modulation(cond, w_mod, b_mod):
    tn = 2048
    out = pl.pallas_call(
        _mod_kernel,
        grid=(DEPTH, 6 * D_MODEL // tn),
        in_specs=[
            pl.BlockSpec((COND_PAD, D_MODEL), lambda l, j: (0, 0)),
            pl.BlockSpec((None, D_MODEL, tn), lambda l, j: (l, 0, j)),
            pl.BlockSpec((None, 1, tn), lambda l, j: (l, 0, j)),
        ],
        out_specs=pl.BlockSpec((None, COND_PAD, tn), lambda l, j: (l, 0, j)),
        out_shape=jax.ShapeDtypeStruct((DEPTH, COND_PAD, 6 * D_MODEL), F32),
        compiler_params=pltpu.CompilerParams(
            dimension_semantics=("arbitrary", "arbitrary"), vmem_limit_bytes=VMEM_LIMIT_BYTES),
        name="modulation",
    )(cond, w_mod, b_mod.reshape(DEPTH, 1, 6 * D_MODEL))
    return out[:, :N_COND].reshape(DEPTH, N_COND, 6, D_MODEL)


def _ffn_kernel(x_ref, xp_ref, xn_ref, mod_ref, g_ref, b_ref, wup_ref, cw_ref, cb_ref, wdn_ref,
                *rest, tm, split_out):
    halo = BF16_ROWS
    out_refs, (hext_ref, aext_ref) = rest[:-2], rest[-2:]
    start = pl.program_id(0) * tm
    length, pos0 = _seq_pos(start)
    m = mod_ref[...]
    shift, scale, gate = m[3:4], m[4:5], m[5:6]

    def modulate(v):
        return (_ln(v) * (1.0 + scale) + shift).astype(BF16)

    x = x_ref[...]
    hext_ref[0:halo] = modulate(xp_ref[...])
    hext_ref[halo:halo + tm] = modulate(x)
    hext_ref[halo + tm:] = modulate(xn_ref[...])

    t = (pos0 + lax.broadcasted_iota(jnp.int32, (tm, 1), 0)) & (length - 1)
    has_prev = t != 0
    has_next = t != length - 1

    acc = jnp.zeros((tm, D_MODEL), F32)
    lo = 0
    for fc in FFN_CHUNKS:
        aext_ref[:, 0:fc] = _dot(hext_ref[...], wup_ref[:, lo:lo + fc])
        bgate = _dot(hext_ref[halo:halo + tm], wup_ref[:, D_FF + lo:D_FF + lo + fc])
        cw = cw_ref[:, lo:lo + fc]
        a = (jnp.where(has_prev, aext_ref[halo - 1:halo - 1 + tm, 0:fc], 0.0) * cw[0:1]
             + aext_ref[halo:halo + tm, 0:fc] * cw[1:2]
             + jnp.where(has_next, aext_ref[halo + 1:halo + 1 + tm, 0:fc], 0.0) * cw[2:3]
             + cb_ref[:, lo:lo + fc])
        act = (jax.nn.gelu(a) * bgate).astype(BF16)
        acc = acc + _dot(act, wdn_ref[lo:lo + fc, :])
        lo += fc
    y = _post_norm(x, acc, gate, g_ref[...], b_ref[...])

    if split_out:
        @pl.when(start < N_P)
        def _():
            out_refs[0][...] = y

        @pl.when(start >= N_P)
        def _():
            out_refs[1][...] = y
    else:
        out_refs[0][...] = y


def _ffn(x, mod, ln_g, ln_b, w_up, conv_w, conv_b, w_down, layer, split_out):
    tm, halo = TM_FFN, BF16_ROWS
    nb = N_TOK // halo
    if split_out:
        out_specs = list(_stream_specs(tm))
        out_shape = [jax.ShapeDtypeStruct((N_P, D_MODEL), F32), jax.ShapeDtypeStruct((N_S, D_MODEL), F32)]
    else:
        out_specs = _row_spec(tm)
        out_shape = jax.ShapeDtypeStruct((N_TOK, D_MODEL), F32)
    return pl.pallas_call(
        functools.partial(_ffn_kernel, tm=tm, split_out=split_out),
        grid=(N_TOK // tm,),
        in_specs=[
            _row_spec(tm),
            pl.BlockSpec((halo, D_MODEL), lambda i: (jnp.maximum(i * (tm // halo) - 1, 0), 0)),
            pl.BlockSpec((halo, D_MODEL), lambda i: (jnp.minimum((i + 1) * (tm // halo), nb - 1), 0)),
            _mod_spec(tm, layer),
            _ln_spec(layer, 1), _ln_spec(layer, 1),
            _layer((D_MODEL, 2 * D_FF), layer),
            _layer((3, D_FF), layer),
            _layer((1, D_FF), layer),
            _layer((D_FF, D_MODEL), layer),
        ],
        out_specs=out_specs,
        out_shape=out_shape,
        scratch_shapes=[
            pltpu.VMEM((tm + 2 * halo, D_MODEL), BF16),
            pltpu.VMEM((tm + 2 * halo, max(FFN_CHUNKS)), F32),
        ],
        compiler_params=_params(),
        name="conv_ffn",
    )(x, x, x, mod, ln_g, ln_b, w_up, conv_w, conv_b, w_down)


def _dft_matrix(n):
    k = np.arange(n, dtype=np.int64)
    ang = 2.0 * np.pi * ((k[:, None] * k[None, :]) % n) / n
    return np.cos(ang).astype(np.float32), np.sin(ang).astype(np.float32)


def _dft_recombine_tables():
    t1 = np.arange(DFT_RADIX, dtype=np.int64)[:, None]
    k = np.arange(DEC_SEQ, dtype=np.int64)[None, :]
    ang = 2.0 * np.pi * ((t1 * k) % DEC_SEQ) / DEC_SEQ
    shape = (DFT_RADIX, DEC_SEQ // DFT_COEF_ROWS, DFT_COEF_ROWS)
    pad = ((0, 0), (0, DFT_COEF_ROWS - shape[1]), (0, 0))
    f = lambda a: np.pad(a.reshape(shape), pad).astype(np.float32)
    return f(np.cos(ang)), f(np.sin(ang))


def _fnet_a_kernel(xp_ref, xs_ref, mod_ref, cs_ref, pp_ref, qp_ref, pd_ref, qd_ref, p_scr, q_scr, *, tm):
    is_prompt = pl.program_id(0) < N_P // tm
    x = jnp.where(is_prompt, xp_ref[...], xs_ref[...])
    m = mod_ref[...]
    h = (_ln(x) * (1.0 + m[1:2]) + m[0:1]).astype(BF16)
    per_group = GROUP_W // LANES
    for g in range(N_GROUPS):
        r = _dot(h[:, g * GROUP_W:(g + 1) * GROUP_W], cs_ref[...])
        for c in range(per_group):
            p_scr[g * per_group + c] = r[:, c * LANES:(c + 1) * LANES]
            q_scr[g * per_group + c] = r[:, GROUP_W + c * LANES:GROUP_W + (c + 1) * LANES]

    @pl.when(is_prompt)
    def _():
        for c in range(D_MODEL // LANES):
            pp_ref[:, c * LANES:(c + 1) * LANES] = p_scr[c].astype(BF16)
            qp_ref[:, c * LANES:(c + 1) * LANES] = q_scr[c].astype(BF16)

    @pl.when(jnp.logical_not(is_prompt))
    def _():
        for t1 in range(DFT_RADIX):
            rows = pl.ds(t1, tm // DFT_RADIX, stride=DFT_RADIX)
            for c in range(D_MODEL // LANES):
                pd_ref[t1, :, c * LANES:(c + 1) * LANES] = p_scr[c, rows, :].astype(BF16)
                qd_ref[t1, :, c * LANES:(c + 1) * LANES] = q_scr[c, rows, :].astype(BF16)


def _fnet_latent_kernel(pd_ref, qd_ref, cs_ref, ct_ref, st_ref, f_ref, acc_ref):
    t1 = pl.program_id(2)

    @pl.when(t1 == 0)
    def _():
        acc_ref[...] = jnp.zeros_like(acc_ref)

    csp = _dot(cs_ref[...], pd_ref[...])
    csq = _dot(cs_ref[...], qd_ref[...])
    u = csp[:DFT_SUB] - csq[DFT_SUB:]
    v = csq[:DFT_SUB] + csp[DFT_SUB:]
    ct, st = ct_ref[...].T, st_ref[...].T
    for a in range(DEC_SEQ // DFT_COEF_ROWS):
        rows = slice(a * DFT_COEF_ROWS, (a + 1) * DFT_COEF_ROWS)
        sub = slice((a * DFT_COEF_ROWS) % DFT_SUB, (a * DFT_COEF_ROWS) % DFT_SUB + DFT_COEF_ROWS)
        acc_ref[rows, :] += ct[:, a:a + 1] * u[sub] - st[:, a:a + 1] * v[sub]

    @pl.when(t1 == DFT_RADIX - 1)
    def _():
        f_ref[...] = (acc_ref[...] * (DEC_SEQ * GROUP_W) ** -0.5).astype(BF16)


def _fnet_b_kernel(xp_ref, xs_ref, pp_ref, qp_ref, fs_ref, c256_ref, s256_ref, wo_ref, mod_ref, g_ref,
                   b_ref, o_ref, f_ref):
    is_prompt = pl.program_id(0) < N_P // TM_SEQ

    @pl.when(is_prompt)
    def _():
        f = _dot(c256_ref[...], pp_ref[...]) - _dot(s256_ref[...], qp_ref[...])
        f_ref[...] = (f * (SEQ * GROUP_W) ** -0.5).astype(BF16)

    @pl.when(jnp.logical_not(is_prompt))
    def _():
        f_ref[...] = fs_ref[...]

    x = jnp.where(is_prompt, xp_ref[...], xs_ref[...])
    y = _dot(f_ref[...], wo_ref[...])
    o_ref[...] = _post_norm(x, y, mod_ref[...][2:3], g_ref[...], b_ref[...])


def _fnet(xp, xs, mod, ln_g, ln_b, w_o, layer, j):
    c256, s256 = _dft_matrix(SEQ)
    cs = jnp.concatenate([jnp.asarray(c256), jnp.asarray(s256)], axis=1).astype(BF16)
    tm = 512
    n_pt = N_P // tm
    per_seq = DEC_SEQ // tm
    prompt_rows = _stream_specs(tm)[0]
    slabs = pl.BlockSpec((None, DFT_RADIX, tm // DFT_RADIX, D_MODEL),
                         lambda i: (jnp.maximum(i - n_pt, 0) // per_seq, 0, jnp.maximum(i - n_pt, 0) % per_seq, 0))
    dec_shape = jax.ShapeDtypeStruct((DEC_BATCH, DFT_RADIX, DFT_SUB, D_MODEL), BF16)
    pp, qp, pd, qd = pl.pallas_call(
        functools.partial(_fnet_a_kernel, tm=tm),
        grid=(N_TOK // tm,),
        in_specs=[*_stream_specs(tm), _mod_spec(tm, layer), _resident((GROUP_W, 2 * GROUP_W))],
        out_specs=[prompt_rows, prompt_rows, slabs, slabs],
        out_shape=[jax.ShapeDtypeStruct((N_P, D_MODEL), BF16)] * 2 + [dec_shape] * 2,
        scratch_shapes=[pltpu.VMEM((D_MODEL // LANES, tm, LANES), F32)] * 2,
        compiler_params=_params(),
        name="fnet_channel_dft",
    )(xp, xs, mod, cs)

    c_sub, s_sub = _dft_matrix(DFT_SUB)
    cs_sub = jnp.concatenate([jnp.asarray(c_sub), jnp.asarray(s_sub)], axis=0).astype(BF16)
    ct, st = _dft_recombine_tables()
    slab = pl.BlockSpec((None, None, DFT_SUB, DFT_CH), lambda b, c, t1: (b, t1, 0, c))
    coef = pl.BlockSpec((None, DFT_COEF_ROWS, DFT_COEF_ROWS), lambda b, c, t1: (t1, 0, 0))
    f_s = pl.pallas_call(
        _fnet_latent_kernel,
        grid=(DEC_BATCH, D_MODEL // DFT_CH, DFT_RADIX),
        in_specs=[slab, slab,
                  pl.BlockSpec((2 * DFT_SUB, DFT_SUB), lambda b, c, t1: (0, 0), pipeline_mode=pl.Buffered(1)),
                  coef, coef],
        out_specs=pl.BlockSpec((DEC_SEQ, DFT_CH), lambda b, c, t1: (b, c)),
        out_shape=jax.ShapeDtypeStruct((N_S, D_MODEL), BF16),
        scratch_shapes=[pltpu.VMEM((DEC_SEQ, DFT_CH), F32)],
        compiler_params=pltpu.CompilerParams(
            dimension_semantics=("arbitrary",) * 3, vmem_limit_bytes=VMEM_LIMIT_BYTES),
        name="fnet_latent_dft",
    )(pd, qd, cs_sub, jnp.asarray(ct), jnp.asarray(st))

    tm = TM_SEQ
    prompt_rows, latent_rows = _stream_specs(tm)
    return pl.pallas_call(
        _fnet_b_kernel,
        grid=(N_TOK // tm,),
        in_specs=[
            *_stream_specs(tm),
            prompt_rows, prompt_rows, latent_rows,
            _resident((SEQ, SEQ)), _resident((SEQ, SEQ)),
            _layer((D_MODEL, D_MODEL), j),
            _mod_spec(tm, layer),
            _ln_spec(layer, 0), _ln_spec(layer, 0),
        ],
        out_specs=_row_spec(tm),
        out_shape=jax.ShapeDtypeStruct((N_TOK, D_MODEL), F32),
        scratch_shapes=[pltpu.VMEM((tm, D_MODEL), BF16)],
        compiler_params=_params(),
        name="fnet_token_dft",
    )(xp, xs, pp, qp, f_s, jnp.asarray(c256).astype(BF16), jnp.asarray(s256).astype(BF16), w_o, mod,
      ln_g, ln_b)


def _rope_tables(tm):
    quarter = HEAD_DIM // 4
    t = np.arange(DEC_SEQ)
    row, col = (t // GRID_W).astype(np.float32), (t % GRID_W).astype(np.float32)
    inv = (np.float32(ROPE_THETA) ** (-np.arange(quarter, dtype=np.float32) / np.float32(quarter))).astype(np.float32)
    ang_r = (row[:, None] * inv[None, :]).astype(np.float32).astype(np.float64)
    ang_c = (col[:, None] * inv[None, :]).astype(np.float32).astype(np.float64)
    cos = np.concatenate([np.cos(ang_r), np.cos(ang_r), np.cos(ang_c), np.cos(ang_c)], axis=1)
    sin = np.concatenate([-np.sin(ang_r), np.sin(ang_r), -np.sin(ang_c), np.sin(ang_c)], axis=1)
    cos = np.concatenate([cos, np.ones((tm, HEAD_DIM))], axis=0)
    sin = np.concatenate([sin, np.zeros((tm, HEAD_DIM))], axis=0)
    return np.asarray(cos, np.float32), np.asarray(sin, np.float32)


def _qkv_kernel(x_ref, mod_ref, w_ref, qg_ref, kg_ref, cos_ref, sin_ref,
                q_ref, k_ref, v_ref, vt_ref, nk_ref, nv_ref, *, tm):
    i = pl.program_id(0)
    m = mod_ref[...]
    h = (_ln(x_ref[...]) * (1.0 + m[1:2]) + m[0:1]).astype(BF16)
    qkv = _dot(h, w_ref[...])
    cos, sin = cos_ref[...], sin_ref[...]
    lane = lax.broadcasted_iota(jnp.int32, (1, HEAD_DIM), 1)
    low = (lane % (HEAD_DIM // 2)) < (HEAD_DIM // 4)

    def rms(v, g):
        return v * lax.rsqrt(jnp.mean(v * v, axis=-1, keepdims=True) + LN_EPS) * g

    def rope(v):
        partner = jnp.where(low, pltpu.roll(v, HEAD_DIM - HEAD_DIM // 4, 1), pltpu.roll(v, HEAD_DIM // 4, 1))
        return v * cos + partner * sin

    is_prompt = i < N_P // tm
    for hq in range(N_Q_HEADS):
        sl = slice(hq * HEAD_DIM, (hq + 1) * HEAD_DIM)
        q_ref[:, sl] = rope(rms(qkv[:, sl], qg_ref[...])).astype(BF16)
    for hk in range(N_KV_HEADS):
        sl = slice(hk * HEAD_DIM, (hk + 1) * HEAD_DIM)
        kn = rms(qkv[:, Q_W + hk * HEAD_DIM:Q_W + (hk + 1) * HEAD_DIM], kg_ref[...])
        k_ref[:, sl] = rope(kn).astype(BF16)

        @pl.when(is_prompt)
        def _():
            nk_ref[:, sl] = kn

    v = qkv[:, Q_W + KV_W:]
    v_ref[...] = v.astype(BF16)

    @pl.when(is_prompt)
    def _():
        nv_ref[...] = v

    @pl.when(jnp.logical_not(is_prompt))
    def _():
        vt_ref[...] = v.T.astype(BF16)


def _attn_kernel(x_ref, q_ref, ks_ref, vs_ref, kb_ref, vbt_ref, kc_ref, vct_ref, wo_ref, mod_ref, g_ref,
                 b_ref, o_ref, att_ref):
    i = pl.program_id(0)
    tm = TM_SEQ
    is_prompt = i < N_P // tm

    def head(hq):
        return slice(hq * HEAD_DIM, (hq + 1) * HEAD_DIM)

    @pl.when(is_prompt)
    def _():
        for hk in range(N_KV_HEADS):
            q4 = jnp.concatenate([q_ref[:, head(hk * Q_PER_KV + g)] for g in range(Q_PER_KV)], axis=0)
            s = _dot_nt(q4, ks_ref[:, head(hk)])
            p = jnp.exp2(s - jnp.max(s, axis=-1, keepdims=True))
            o4 = _dot(p.astype(BF16), vs_ref[:, head(hk)]) / jnp.sum(p, axis=-1, keepdims=True)
            for g in range(Q_PER_KV):
                att_ref[:, head(hk * Q_PER_KV + g)] = o4[g * tm:(g + 1) * tm].astype(BF16)

    @pl.when(jnp.logical_not(is_prompt))
    def _():
        for hk in range(N_KV_HEADS):
            for g0 in range(0, Q_PER_KV, ATTN_PAIR):
                hq0 = hk * Q_PER_KV + g0
                qq = jnp.concatenate([q_ref[:, head(hq0 + g)] for g in range(ATTN_PAIR)], axis=0)
                st_b = _dot_nt(kb_ref[:, head(hk)], qq)
                st_c = _dot_nt(kc_ref[:, head(hk)], qq)
                mx = jnp.maximum(jnp.max(st_b, axis=0, keepdims=True), jnp.max(st_c, axis=0, keepdims=True))
                pt_b = jnp.exp2(st_b - mx)
                pt_c = jnp.exp2(st_c - mx)
                den = jnp.sum(pt_b, axis=0, keepdims=True) + jnp.sum(pt_c, axis=0, keepdims=True)
                ot = (_dot(vbt_ref[head(hk), :], pt_b.astype(BF16))
                      + _dot(vct_ref[head(hk), :], pt_c.astype(BF16))) / den
                oo = ot.T
                for g in range(ATTN_PAIR):
                    att_ref[:, head(hq0 + g)] = oo[g * tm:(g + 1) * tm].astype(BF16)

    y = _dot(att_ref[...], wo_ref[...])
    o_ref[...] = _post_norm(x_ref[...], y, mod_ref[...][2:3], g_ref[...], b_ref[...])


def _attention(x, mod, ln_g, ln_b, w_qkv, q_norm, k_norm, w_o, cache_k, cache_v, layer, j):
    tm = TM_QKV
    n_pt = N_P // tm
    cos, sin = _rope_tables(tm)
    rope_spec = pl.BlockSpec(
        (tm, HEAD_DIM),
        lambda i: (jnp.where(i < n_pt, DEC_SEQ // tm, jnp.maximum(i - n_pt, 0) % (DEC_SEQ // tm)), 0))
    prompt_row = _stream_specs(tm, KV_W)[0]
    latent_col = pl.BlockSpec((KV_W, tm), lambda i: (0, jnp.maximum(i - n_pt, 0)))
    q_gain = q_norm[j] * (math.log2(math.e) * HEAD_DIM ** -0.5)
    q, k, v, vt, new_k, new_v = pl.pallas_call(
        functools.partial(_qkv_kernel, tm=tm),
        grid=(N_TOK // tm,),
        in_specs=[
            _row_spec(tm),
            _mod_spec(tm, layer),
            _layer((D_MODEL, QKV_W), j),
            _resident((1, HEAD_DIM)), _resident((1, HEAD_DIM)),
            rope_spec, rope_spec,
        ],
        out_specs=[_row_spec(tm, Q_W), _row_spec(tm, KV_W), _row_spec(tm, KV_W), latent_col,
                   prompt_row, prompt_row],
        out_shape=[
            jax.ShapeDtypeStruct((N_TOK, Q_W), BF16),
            jax.ShapeDtypeStruct((N_TOK, KV_W), BF16),
            jax.ShapeDtypeStruct((N_TOK, KV_W), BF16),
            jax.ShapeDtypeStruct((KV_W, N_S), BF16),
            jax.ShapeDtypeStruct((N_P, KV_W), F32),
            jax.ShapeDtypeStruct((N_P, KV_W), F32),
        ],
        compiler_params=_params(),
        name="qkv_rope",
    )(x, mod, w_qkv, q_gain.reshape(1, HEAD_DIM), k_norm[j].reshape(1, HEAD_DIM), jnp.asarray(cos),
      jnp.asarray(sin))

    tm = TM_SEQ
    n_pt = N_P // tm
    per_seq = DEC_SEQ // tm
    batch_of = lambda i: jnp.maximum(i - n_pt, 0) // per_seq
    small = _stream_specs(tm, KV_W)[0]
    k_big = pl.BlockSpec((DEC_SEQ, KV_W), lambda i: (N_P // DEC_SEQ + batch_of(i), 0))
    vt_big = pl.BlockSpec((KV_W, DEC_SEQ), lambda i: (0, batch_of(i)))
    kc_spec = pl.BlockSpec((None, PAST_LEN, KV_W), lambda i: (batch_of(i), 0, 0))
    vct_spec = pl.BlockSpec((None, KV_W, PAST_LEN), lambda i: (batch_of(i), 0, 0))
    kc = cache_k[:, j].reshape(DEC_BATCH, PAST_LEN, KV_W).astype(BF16)
    vct = cache_v[:, j].reshape(DEC_BATCH, PAST_LEN, KV_W).transpose(0, 2, 1).astype(BF16)
    x_new = pl.pallas_call(
        _attn_kernel,
        grid=(N_TOK // tm,),
        in_specs=[
            _row_spec(tm), _row_spec(tm, Q_W),
            small, small, k_big, vt_big, kc_spec, vct_spec,
            _layer((Q_W, D_MODEL), j),
            _mod_spec(tm, layer),
            _ln_spec(layer, 0), _ln_spec(layer, 0),
        ],
        out_specs=_row_spec(tm),
        out_shape=jax.ShapeDtypeStruct((N_TOK, D_MODEL), F32),
        scratch_shapes=[pltpu.VMEM((tm, Q_W), BF16)],
        compiler_params=_params(),
        name="attention",
    )(x, q, k, v, k, vt, kc, vct, w_o, mod, ln_g, ln_b)
    return x_new, new_k, new_v


POOL_HALO = 8


def _pool_kernel(x_ref, xp_ref, xn_ref, mod_ref, w_ref, sc_ref, g_ref, b_ref, o_ref, buf_ref, p_ref):
    tm, halo = TM_SEQ, POOL_HALO
    start = pl.program_id(0) * tm
    length, pos0 = _seq_pos(start)
    m = mod_ref[...]
    shift, scale, gate = m[0:1], m[1:2], m[2:3]

    def modulate(v):
        return _ln(v) * (1.0 + scale) + shift

    x = x_ref[...]
    buf_ref[0:halo] = jnp.where(pos0 != 0, modulate(xp_ref[...]), 0.0)
    buf_ref[halo:halo + tm] = modulate(x)
    buf_ref[halo + tm:] = jnp.where(pos0 + tm != length, modulate(xn_ref[...]), 0.0)

    t = pos0 + lax.broadcasted_iota(jnp.int32, (tm, 1), 0)
    for g, half in enumerate(POOL_HALF):
        cols = slice(g * GROUP_W, (g + 1) * GROUP_W)
        wsum = buf_ref[halo - half:halo - half + tm, cols]
        for j in range(1 - half, half):
            wsum = wsum + buf_ref[halo + j:halo + j + tm, cols]
        cnt = (jnp.minimum(t + half, length) - jnp.maximum(t - half, 0)).astype(F32)
        pooled = wsum / cnt - buf_ref[halo:halo + tm, cols]
        p_ref[:, cols] = _dot(pooled.astype(BF16), w_ref[g])
    y = p_ref[...] * sc_ref[...]
    o_ref[...] = _post_norm(x, y, gate, g_ref[...], b_ref[...])


def _pool(x, mod, ln_g, ln_b, w_grp, scale, layer, j):
    tm, halo = TM_SEQ, POOL_HALO
    nb = N_TOK // halo
    return pl.pallas_call(
        _pool_kernel,
        grid=(N_TOK // tm,),
        in_specs=[
            _row_spec(tm),
            pl.BlockSpec((halo, D_MODEL), lambda i: (jnp.maximum(i * (tm // halo) - 1, 0), 0)),
            pl.BlockSpec((halo, D_MODEL), lambda i: (jnp.minimum((i + 1) * (tm // halo), nb - 1), 0)),
            _mod_spec(tm, layer),
            _layer((N_GROUPS, GROUP_W, GROUP_W), j),
            _layer((1, D_MODEL), j),
            _ln_spec(layer, 0), _ln_spec(layer, 0),
        ],
        out_specs=_row_spec(tm),
        out_shape=jax.ShapeDtypeStruct((N_TOK, D_MODEL), F32),
        scratch_shapes=[
            pltpu.VMEM((tm + 2 * halo, D_MODEL), F32),
            pltpu.VMEM((tm, D_MODEL), F32),
        ],
        compiler_params=_params(),
        name="pool_mixer",
    )(x, x, x, mod, w_grp, scale, ln_g, ln_b)


def _gmlp_kernel(x_ref, mod_ref, win_ref, bin_ref, vg_ref, vb_ref, ws_ref, bs_ref, wo_ref, g_ref, b_ref,
                 o_ref, act_ref, *, tm):
    m = mod_ref[...]
    x = x_ref[...]
    h = (_ln(x) * (1.0 + m[1:2]) + m[0:1]).astype(BF16)
    z = jax.nn.gelu(_dot(h, win_ref[...]) + bin_ref[...])
    u = z[:, :GMLP_W]
    v = (_ln(z[:, GMLP_W:]) * vg_ref[...] + vb_ref[...]).astype(BF16)
    bs = bs_ref[...]
    for n in range(tm // CHUNK):
        rows = slice(n * CHUNK, (n + 1) * CHUNK)
        for g in range(GMLP_GROUPS):
            cols = slice(g * GMLP_GW, (g + 1) * GMLP_GW)
            s = _dot(ws_ref[g], v[rows, cols]) + bs[:, g:g + 1]
            act_ref[rows, cols] = (u[rows, cols] * s).astype(BF16)
    y = _dot(act_ref[...], wo_ref[...])
    o_ref[...] = _post_norm(x, y, m[2:3], g_ref[...], b_ref[...])


def _gmlp(x, mod, ln_g, ln_b, w_in, b_in, v_g, v_b, w_s, b_s_t, w_o, layer, j):
    tm = TM_GMLP
    return pl.pallas_call(
        functools.partial(_gmlp_kernel, tm=tm),
        grid=(N_TOK // tm,),
        in_specs=[
            _row_spec(tm),
            _mod_spec(tm, layer),
            _layer((D_MODEL, 2 * GMLP_W), j),
            _layer((1, 2 * GMLP_W), j),
            _layer((1, GMLP_W), j), _layer((1, GMLP_W), j),
            _layer((GMLP_GROUPS, CHUNK, CHUNK), j),
            _layer((CHUNK, GMLP_GROUPS), j),
            _layer((GMLP_W, D_MODEL), j),
            _ln_spec(layer, 0), _ln_spec(layer, 0),
        ],
        out_specs=_row_spec(tm),
        out_shape=jax.ShapeDtypeStruct((N_TOK, D_MODEL), F32),
        scratch_shapes=[pltpu.VMEM((tm, GMLP_W), BF16)],
        compiler_params=_params(),
        name="gmlp_mixer",
    )(x, mod, w_in, b_in, v_g, v_b, w_s, b_s_t, w_o, ln_g, ln_b)


def kernel(x_prompt, x_sample, cache_k, cache_v, c, c_ctx, w_mod, b_mod, ln_g, ln_b, ffn_w_up, ffn_conv_w,
           ffn_conv_b, ffn_w_down, fnet_w_o, attn_w_qkv, attn_q_norm, attn_k_norm, attn_w_o, pool_w,
           pool_scale, gmlp_w_in, gmlp_b_in, gmlp_ln_g, gmlp_ln_b, gmlp_w_s, gmlp_b_s, gmlp_w_o):
    assert DEPTH == 4
    cond = jnp.concatenate([c_ctx[None, :], c, jnp.zeros((COND_PAD - N_COND, D_MODEL), F32)], axis=0)
    mod = _modulation(cond, w_mod, b_mod)
    ln_g = ln_g.reshape(DEPTH * 2, 1, D_MODEL)
    ln_b = ln_b.reshape(DEPTH * 2, 1, D_MODEL)
    ffn = (ffn_w_up.astype(BF16), ffn_conv_w, ffn_conv_b.reshape(DEPTH, 1, D_FF), ffn_w_down.astype(BF16))

    x = _fnet(x_prompt.reshape(N_P, D_MODEL), x_sample.reshape(N_S, D_MODEL), mod, ln_g, ln_b,
              fnet_w_o.astype(BF16), 0, 0)
    x = _ffn(x, mod, ln_g, ln_b, *ffn, 0, False)
    x, new_k, new_v = _attention(x, mod, ln_g, ln_b, attn_w_qkv.astype(BF16), attn_q_norm, attn_k_norm,
                                 attn_w_o.astype(BF16), cache_k, cache_v, 1, 0)
    x = _ffn(x, mod, ln_g, ln_b, *ffn, 1, False)
    x = _pool(x, mod, ln_g, ln_b, pool_w.astype(BF16), pool_scale.reshape(-1, 1, D_MODEL), 2, 0)
    x = _ffn(x, mod, ln_g, ln_b, *ffn, 2, False)
    x = _gmlp(x, mod, ln_g, ln_b, gmlp_w_in.astype(BF16), gmlp_b_in.reshape(-1, 1, 2 * GMLP_W),
              gmlp_ln_g.reshape(-1, 1, GMLP_W), gmlp_ln_b.reshape(-1, 1, GMLP_W), gmlp_w_s.astype(BF16),
              gmlp_b_s.transpose(0, 2, 1), gmlp_w_o.astype(BF16), 3, 0)
    y_prompt, y_sample = _ffn(x, mod, ln_g, ln_b, *ffn, 3, True)

    return (y_prompt.reshape(BATCH, SEQ, D_MODEL), y_sample.reshape(DEC_BATCH, DEC_SEQ, D_MODEL),
            new_k.reshape(BATCH, 1, SEQ, N_KV_HEADS, HEAD_DIM), new_v.reshape(BATCH, 1, SEQ, N_KV_HEADS, HEAD_DIM))
```

```python
import functools
import math

import numpy as np
import jax
import jax.numpy as jnp
from jax import lax
from jax.experimental import pallas as pl
from jax.experimental.pallas import tpu as pltpu

D_MODEL = 1024
BATCH = 32
SEQ = 256
DEPTH = 4
DEC_BATCH = 2
DEC_SEQ = 4096
PAST_LEN = 512
GRID_W = 64
N_GROUPS = 4
GROUP_W = D_MODEL // N_GROUPS
HEAD_DIM = 128
N_Q_HEADS = D_MODEL // HEAD_DIM
N_KV_HEADS = 2
Q_PER_KV = N_Q_HEADS // N_KV_HEADS
Q_W = N_Q_HEADS * HEAD_DIM
KV_W = N_KV_HEADS * HEAD_DIM
QKV_W = Q_W + 2 * KV_W
ROPE_THETA = 10000.0
POOL_HALF = (1, 2, 4, 8)
CHUNK = 128
GMLP_W = 2 * D_MODEL
GMLP_GROUPS = 4
GMLP_GW = GMLP_W // GMLP_GROUPS
D_FF = ((8 * D_MODEL // 3 + 127) // 128) * 128
ALPHA = (2 * DEPTH) ** 0.25
LN_EPS = 1e-6

N_P = BATCH * SEQ
N_S = DEC_BATCH * DEC_SEQ
N_TOK = N_P + N_S
N_COND = 1 + DEC_BATCH
COND_PAD = 8

F32 = jnp.float32
BF16 = jnp.bfloat16

VMEM_LIMIT_BYTES = 56 * 1024 * 1024
BF16_ROWS = 16

TM_SEQ = SEQ
TM_FFN = 512
TM_QKV = 512
TM_GMLP = 512
LANES = 128
MXU_DIM = 256
FFN_CHUNKS = (5 * MXU_DIM, 6 * MXU_DIM)
assert sum(FFN_CHUNKS) == D_FF
DFT_RADIX = 8
DFT_SUB = DEC_SEQ // DFT_RADIX
DFT_COEF_ROWS = 128
DFT_CH = 512
ATTN_PAIR = 2


def _dot(a, b):
    return jnp.dot(a, b, preferred_element_type=F32)


def _dot_nt(a, b):
    return lax.dot_general(a, b, (((1,), (1,)), ((), ())), preferred_element_type=F32)


def _ln(x):
    mu = jnp.mean(x, axis=-1, keepdims=True)
    xc = x - mu
    var = jnp.mean(xc * xc, axis=-1, keepdims=True)
    return xc * lax.rsqrt(var + LN_EPS)


def _post_norm(x, y, gate, g, b):
    return _ln(ALPHA * x + gate * y) * g + b


def _cond_row(row_start):
    return jnp.maximum(row_start - (N_P - DEC_SEQ), 0) // DEC_SEQ


def _seq_pos(row_start):
    is_p = row_start < N_P
    length = jnp.where(is_p, SEQ, DEC_SEQ)
    pos = jnp.where(is_p, row_start, row_start - N_P) & (length - 1)
    return length, pos


def _params():
    return pltpu.CompilerParams(dimension_semantics=("arbitrary",), vmem_limit_bytes=VMEM_LIMIT_BYTES)


def _resident(shape):
    return pl.BlockSpec(shape, lambda i: (0,) * len(shape), pipeline_mode=pl.Buffered(1))


def _layer(tail, layer):
    return pl.BlockSpec((None,) + tuple(tail), lambda i: (layer,) + (0,) * len(tail),
                        pipeline_mode=pl.Buffered(1))


def _mod_spec(tm, layer):
    return pl.BlockSpec((None, None, 6, D_MODEL), lambda i: (layer, _cond_row(i * tm), 0, 0))


def _ln_spec(layer, which):
    return pl.BlockSpec((None, 1, D_MODEL), lambda i: (2 * layer + which, 0, 0),
                        pipeline_mode=pl.Buffered(1))


def _row_spec(tm, width=D_MODEL):
    return pl.BlockSpec((tm, width), lambda i: (i, 0))


def _stream_specs(tm, width=D_MODEL):
    n_pt = N_P // tm
    return (pl.BlockSpec((tm, width), lambda i: (jnp.minimum(i, n_pt - 1), 0)),
            pl.BlockSpec((tm, width), lambda i: (jnp.maximum(i - n_pt, 0), 0)))


def _mod_kernel(c_ref, w_ref, b_ref, o_ref):
    c = c_ref[...]
    s = c / (1.0 + jnp.exp(-c))
    o_ref[...] = _dot(s.astype(BF16), w_ref[...].astype(BF16)) + b_ref[...]


def _modulation(cond, w_mod, b_mod):
    tn = 2048
    out = pl.pallas_call(
        _mod_kernel,
        grid=(DEPTH, 6 * D_MODEL // tn),
        in_specs=[
            pl.BlockSpec((COND_PAD, D_MODEL), lambda l, j: (0, 0)),
            pl.BlockSpec((None, D_MODEL, tn), lambda l, j: (l, 0, j)),
            pl.BlockSpec((None, 1, tn), lambda l, j: (l, 0, j)),
        ],
        out_specs=pl.BlockSpec((None, COND_PAD, tn), lambda l, j: (l, 0, j)),
        out_shape=jax.ShapeDtypeStruct((DEPTH, COND_PAD, 6 * D_MODEL), F32),
        compiler_params=pltpu.CompilerParams(
            dimension_semantics=("arbitrary", "arbitrary"), vmem_limit_bytes=VMEM_LIMIT_BYTES),
        name="modulation",
    )(cond, w_mod, b_mod.reshape(DEPTH, 1, 6 * D_MODEL))
    return out[:, :N_COND].reshape(DEPTH, N_COND, 6, D_MODEL)


def _ffn_kernel(x_ref, xp_ref, xn_ref, mod_ref, g_ref, b_ref, wup_ref, cw_ref, cb_ref, wdn_ref,
                *rest, tm, split_out):
    halo = BF16_ROWS
    out_refs, (hext_ref, aext_ref) = rest[:-2], rest[-2:]
    start = pl.program_id(0) * tm
    length, pos0 = _seq_pos(start)
    m = mod_ref[...]
    shift, scale, gate = m[3:4], m[4:5], m[5:6]

    def modulate(v):
        return (_ln(v) * (1.0 + scale) + shift).astype(BF16)

    x = x_ref[...]
    hext_ref[0:halo] = modulate(xp_ref[...])
    hext_ref[halo:halo + tm] = modulate(x)
    hext_ref[halo + tm:] = modulate(xn_ref[...])

    t = (pos0 + lax.broadcasted_iota(jnp.int32, (tm, 1), 0)) & (length - 1)
    has_prev = t != 0
    has_next = t != length - 1

    acc = jnp.zeros((tm, D_MODEL), F32)
    lo = 0
    for fc in FFN_CHUNKS:
        aext_ref[:, 0:fc] = _dot(hext_ref[...], wup_ref[:, lo:lo + fc])
        bgate = _dot(hext_ref[halo:halo + tm], wup_ref[:, D_FF + lo:D_FF + lo + fc])
        cw = cw_ref[:, lo:lo + fc]
        a = (jnp.where(has_prev, aext_ref[halo - 1:halo - 1 + tm, 0:fc], 0.0) * cw[0:1]
             + aext_ref[halo:halo + tm, 0:fc] * cw[1:2]
             + jnp.where(has_next, aext_ref[halo + 1:halo + 1 + tm, 0:fc], 0.0) * cw[2:3]
             + cb_ref[:, lo:lo + fc])
        act = (jax.nn.gelu(a) * bgate).astype(BF16)
        acc = acc + _dot(act, wdn_ref[lo:lo + fc, :])
        lo += fc
    y = _post_norm(x, acc, gate, g_ref[...], b_ref[...])

    if split_out:
        @pl.when(start < N_P)
        def _():
            out_refs[0][...] = y

        @pl.when(start >= N_P)
        def _():
            out_refs[1][...] = y
    else:
        out_refs[0][...] = y


def _ffn(x, mod, ln_g, ln_b, w_up, conv_w, conv_b, w_down, layer, split_out):
    tm, halo = TM_FFN, BF16_ROWS
    nb = N_TOK // halo
    if split_out:
        out_specs = list(_stream_specs(tm))
        out_shape = [jax.ShapeDtypeStruct((N_P, D_MODEL), F32), jax.ShapeDtypeStruct((N_S, D_MODEL), F32)]
    else:
        out_specs = _row_spec(tm)
        out_shape = jax.ShapeDtypeStruct((N_TOK, D_MODEL), F32)
    return pl.pallas_call(
        functools.partial(_ffn_kernel, tm=tm, split_out=split_out),
        grid=(N_TOK // tm,),
        in_specs=[
            _row_spec(tm),
            pl.BlockSpec((halo, D_MODEL), lambda i: (jnp.maximum(i * (tm // halo) - 1, 0), 0)),
            pl.BlockSpec((halo, D_MODEL), lambda i: (jnp.minimum((i + 1) * (tm // halo), nb - 1), 0)),
            _mod_spec(tm, layer),
            _ln_spec(layer, 1), _ln_spec(layer, 1),
            _layer((D_MODEL, 2 * D_FF), layer),
            _layer((3, D_FF), layer),
            _layer((1, D_FF), layer),
            _layer((D_FF, D_MODEL), layer),
        ],
        out_specs=out_specs,
        out_shape=out_shape,
        scratch_shapes=[
            pltpu.VMEM((tm + 2 * halo, D_MODEL), BF16),
            pltpu.VMEM((tm + 2 * halo, max(FFN_CHUNKS)), F32),
        ],
        compiler_params=_params(),
        name="conv_ffn",
    )(x, x, x, mod, ln_g, ln_b, w_up, conv_w, conv_b, w_down)


def _dft_matrix(n):
    k = np.arange(n, dtype=np.int64)
    ang = 2.0 * np.pi * ((k[:, None] * k[None, :]) % n) / n
    return np.cos(ang).astype(np.float32), np.sin(ang).astype(np.float32)


def _dft_twiddle_tables():
    t1 = np.arange(DFT_RADIX, dtype=np.int64)[:, None]
    k2 = np.arange(DFT_SUB, dtype=np.int64)[None, :]
    ang = 2.0 * np.pi * (t1 * k2) / DEC_SEQ
    shape = (DFT_RADIX, DFT_SUB // DFT_COEF_ROWS, DFT_COEF_ROWS)
    pad = ((0, 0), (0, DFT_COEF_ROWS - shape[1]), (0, 0))
    f = lambda a: np.pad(a.reshape(shape), pad).astype(np.float32)
    return f(np.cos(ang)), f(np.sin(ang))


def _octant(n):
    r = math.sqrt(0.5)
    return [(1.0, 0.0), (r, r), (0.0, 1.0), (-r, r), (-1.0, 0.0), (-r, -r), (0.0, -1.0), (r, -r)][n % 8]


def _fnet_a_kernel(xp_ref, xs_ref, mod_ref, cs_ref, pp_ref, qp_ref, pd_ref, qd_ref, p_scr, q_scr, *, tm):
    is_prompt = pl.program_id(0) < N_P // tm
    x = jnp.where(is_prompt, xp_ref[...], xs_ref[...])
    m = mod_ref[...]
    h = (_ln(x) * (1.0 + m[1:2]) + m[0:1]).astype(BF16)
    per_group = GROUP_W // LANES
    for g in range(N_GROUPS):
        r = _dot(h[:, g * GROUP_W:(g + 1) * GROUP_W], cs_ref[...])
        for c in range(per_group):
            p_scr[g * per_group + c] = r[:, c * LANES:(c + 1) * LANES]
            q_scr[g * per_group + c] = r[:, GROUP_W + c * LANES:GROUP_W + (c + 1) * LANES]

    @pl.when(is_prompt)
    def _():
        for c in range(D_MODEL // LANES):
            pp_ref[:, c * LANES:(c + 1) * LANES] = p_scr[c].astype(BF16)
            qp_ref[:, c * LANES:(c + 1) * LANES] = q_scr[c].astype(BF16)

    @pl.when(jnp.logical_not(is_prompt))
    def _():
        for t1 in range(DFT_RADIX):
            rows = pl.ds(t1, tm // DFT_RADIX, stride=DFT_RADIX)
            for c in range(D_MODEL // LANES):
                pd_ref[t1, :, c * LANES:(c + 1) * LANES] = p_scr[c, rows, :].astype(BF16)
                qd_ref[t1, :, c * LANES:(c + 1) * LANES] = q_scr[c, rows, :].astype(BF16)


def _fnet_latent_kernel(pd_ref, qd_ref, cs_ref, ct_ref, st_ref, f_ref, a_ref, b_ref):
    for t1 in range(DFT_RADIX):
        csp = _dot(cs_ref[...], pd_ref[t1])
        csq = _dot(cs_ref[...], qd_ref[t1])
        u = csp[:DFT_SUB] - csq[DFT_SUB:]
        v = csq[:DFT_SUB] + csp[DFT_SUB:]
        if t1 == 0:
            a_ref[0], b_ref[0] = u, v
            continue
        ct, st = ct_ref[t1].T, st_ref[t1].T
        for blk in range(DFT_SUB // DFT_COEF_ROWS):
            rows = slice(blk * DFT_COEF_ROWS, (blk + 1) * DFT_COEF_ROWS)
            cc, ss = ct[:, blk:blk + 1], st[:, blk:blk + 1]
            a_ref[t1, rows, :] = cc * u[rows] - ss * v[rows]
            b_ref[t1, rows, :] = ss * u[rows] + cc * v[rows]

    norm = (DEC_SEQ * GROUP_W) ** -0.5
    for k1 in range(DFT_RADIX):
        for blk in range(DFT_SUB // DFT_COEF_ROWS):
            rows = slice(blk * DFT_COEF_ROWS, (blk + 1) * DFT_COEF_ROWS)
            acc = a_ref[0, rows, :]
            for t1 in range(1, DFT_RADIX):
                c, s = _octant(k1 * t1)
                if s == 0.0:
                    acc = acc + a_ref[t1, rows, :] if c > 0 else acc - a_ref[t1, rows, :]
                elif c == 0.0:
                    acc = acc - b_ref[t1, rows, :] if s > 0 else acc + b_ref[t1, rows, :]
                else:
                    acc = acc + (c * a_ref[t1, rows, :] - s * b_ref[t1, rows, :])
            out_rows = slice(k1 * DFT_SUB + blk * DFT_COEF_ROWS, k1 * DFT_SUB + (blk + 1) * DFT_COEF_ROWS)
            f_ref[out_rows, :] = (acc * norm).astype(BF16)


def _fnet_b_kernel(xp_ref, xs_ref, pp_ref, qp_ref, fs_ref, c256_ref, s256_ref, wo_ref, mod_ref, g_ref,
                   b_ref, o_ref, f_ref, *, tm):
    is_prompt = pl.program_id(0) < N_P // tm

    @pl.when(is_prompt)
    def _():
        for s in range(tm // SEQ):
            rows = slice(s * SEQ, (s + 1) * SEQ)
            f = _dot(c256_ref[...], pp_ref[rows, :]) - _dot(s256_ref[...], qp_ref[rows, :])
            f_ref[rows, :] = (f * (SEQ * GROUP_W) ** -0.5).astype(BF16)

    @pl.when(jnp.logical_not(is_prompt))
    def _():
        f_ref[...] = fs_ref[...]

    x = jnp.where(is_prompt, xp_ref[...], xs_ref[...])
    y = _dot(f_ref[...], wo_ref[...])
    o_ref[...] = _post_norm(x, y, mod_ref[...][2:3], g_ref[...], b_ref[...])


def _fnet(xp, xs, mod, ln_g, ln_b, w_o, layer, j):
    c256, s256 = _dft_matrix(SEQ)
    cs = jnp.concatenate([jnp.asarray(c256), jnp.asarray(s256)], axis=1).astype(BF16)
    tm = 512
    n_pt = N_P // tm
    per_seq = DEC_SEQ // tm
    prompt_rows = _stream_specs(tm)[0]
    slabs = pl.BlockSpec((None, DFT_RADIX, tm // DFT_RADIX, D_MODEL),
                         lambda i: (jnp.maximum(i - n_pt, 0) // per_seq, 0, jnp.maximum(i - n_pt, 0) % per_seq, 0))
    dec_shape = jax.ShapeDtypeStruct((DEC_BATCH, DFT_RADIX, DFT_SUB, D_MODEL), BF16)
    pp, qp, pd, qd = pl.pallas_call(
        functools.partial(_fnet_a_kernel, tm=tm),
        grid=(N_TOK // tm,),
        in_specs=[*_stream_specs(tm), _mod_spec(tm, layer), _resident((GROUP_W, 2 * GROUP_W))],
        out_specs=[prompt_rows, prompt_rows, slabs, slabs],
        out_shape=[jax.ShapeDtypeStruct((N_P, D_MODEL), BF16)] * 2 + [dec_shape] * 2,
        scratch_shapes=[pltpu.VMEM((D_MODEL // LANES, tm, LANES), F32)] * 2,
        compiler_params=_params(),
        name="fnet_channel_dft",
    )(xp, xs, mod, cs)

    c_sub, s_sub = _dft_matrix(DFT_SUB)
    cs_sub = jnp.concatenate([jnp.asarray(c_sub), jnp.asarray(s_sub)], axis=0).astype(BF16)
    ct, st = _dft_twiddle_tables()
    slabs_in = pl.BlockSpec((None, DFT_RADIX, DFT_SUB, DFT_CH), lambda b, c: (b, 0, 0, c))
    whole = lambda shape: pl.BlockSpec(shape, lambda b, c: (0,) * len(shape), pipeline_mode=pl.Buffered(1))
    f_s = pl.pallas_call(
        _fnet_latent_kernel,
        grid=(DEC_BATCH, D_MODEL // DFT_CH),
        in_specs=[slabs_in, slabs_in, whole((2 * DFT_SUB, DFT_SUB)), whole(ct.shape), whole(st.shape)],
        out_specs=pl.BlockSpec((DEC_SEQ, DFT_CH), lambda b, c: (b, c)),
        out_shape=jax.ShapeDtypeStruct((N_S, D_MODEL), BF16),
        scratch_shapes=[pltpu.VMEM((DFT_RADIX, DFT_SUB, DFT_CH), F32)] * 2,
        compiler_params=pltpu.CompilerParams(
            dimension_semantics=("arbitrary",) * 2, vmem_limit_bytes=VMEM_LIMIT_BYTES),
        name="fnet_latent_dft",
    )(pd, qd, cs_sub, jnp.asarray(ct), jnp.asarray(st))

    prompt_rows, latent_rows = _stream_specs(tm)
    return pl.pallas_call(
        functools.partial(_fnet_b_kernel, tm=tm),
        grid=(N_TOK // tm,),
        in_specs=[
            *_stream_specs(tm),
            prompt_rows, prompt_rows, latent_rows,
            _resident((SEQ, SEQ)), _resident((SEQ, SEQ)),
            _layer((D_MODEL, D_MODEL), j),
            _mod_spec(tm, layer),
            _ln_spec(layer, 0), _ln_spec(layer, 0),
        ],
        out_specs=_row_spec(tm),
        out_shape=jax.ShapeDtypeStruct((N_TOK, D_MODEL), F32),
        scratch_shapes=[pltpu.VMEM((tm, D_MODEL), BF16)],
        compiler_params=_params(),
        name="fnet_token_dft",
    )(xp, xs, pp, qp, f_s, jnp.asarray(c256).astype(BF16), jnp.asarray(s256).astype(BF16), w_o, mod,
      ln_g, ln_b)


def _rope_tables(tm):
    quarter = HEAD_DIM // 4
    t = np.arange(DEC_SEQ)
    row, col = (t // GRID_W).astype(np.float32), (t % GRID_W).astype(np.float32)
    inv = (np.float32(ROPE_THETA) ** (-np.arange(quarter, dtype=np.float32) / np.float32(quarter))).astype(np.float32)
    ang_r = (row[:, None] * inv[None, :]).astype(np.float32).astype(np.float64)
    ang_c = (col[:, None] * inv[None, :]).astype(np.float32).astype(np.float64)
    cos = np.concatenate([np.cos(ang_r), np.cos(ang_r), np.cos(ang_c), np.cos(ang_c)], axis=1)
    sin = np.concatenate([-np.sin(ang_r), np.sin(ang_r), -np.sin(ang_c), np.sin(ang_c)], axis=1)
    cos = np.concatenate([cos, np.ones((tm, HEAD_DIM))], axis=0)
    sin = np.concatenate([sin, np.zeros((tm, HEAD_DIM))], axis=0)
    return np.asarray(cos, np.float32), np.asarray(sin, np.float32)


def _qkv_kernel(x_ref, mod_ref, w_ref, qg_ref, kg_ref, cos_ref, sin_ref,
                q_ref, k_ref, v_ref, vt_ref, nk_ref, nv_ref, *, tm):
    i = pl.program_id(0)
    m = mod_ref[...]
    h = (_ln(x_ref[...]) * (1.0 + m[1:2]) + m[0:1]).astype(BF16)
    cos, sin = cos_ref[...], sin_ref[...]
    lane = lax.broadcasted_iota(jnp.int32, (1, HEAD_DIM), 1)
    low = (lane % (HEAD_DIM // 2)) < (HEAD_DIM // 4)

    def rms(v, g):
        return v * lax.rsqrt(jnp.mean(v * v, axis=-1, keepdims=True) + LN_EPS) * g

    def rope(v):
        partner = jnp.where(low, pltpu.roll(v, HEAD_DIM - HEAD_DIM // 4, 1), pltpu.roll(v, HEAD_DIM // 4, 1))
        return v * cos + partner * sin

    def project(lo):
        return _dot(h, w_ref[:, lo:lo + MXU_DIM])

    heads_per_block = MXU_DIM // HEAD_DIM
    for blk in range(Q_W // MXU_DIM):
        qq = project(blk * MXU_DIM)
        for j in range(heads_per_block):
            sl = slice((blk * heads_per_block + j) * HEAD_DIM, (blk * heads_per_block + j + 1) * HEAD_DIM)
            q_ref[:, sl] = rope(rms(qq[:, j * HEAD_DIM:(j + 1) * HEAD_DIM], qg_ref[...])).astype(BF16)
    kk = project(Q_W)
    kn = [rms(kk[:, hk * HEAD_DIM:(hk + 1) * HEAD_DIM], kg_ref[...]) for hk in range(N_KV_HEADS)]
    for hk in range(N_KV_HEADS):
        k_ref[:, hk * HEAD_DIM:(hk + 1) * HEAD_DIM] = rope(kn[hk]).astype(BF16)
    v = project(Q_W + KV_W)
    v_ref[...] = v.astype(BF16)

    is_prompt = i < N_P // tm

    @pl.when(is_prompt)
    def _():
        for hk in range(N_KV_HEADS):
            nk_ref[:, hk * HEAD_DIM:(hk + 1) * HEAD_DIM] = kn[hk]
        nv_ref[...] = v

    @pl.when(jnp.logical_not(is_prompt))
    def _():
        vt_ref[...] = v.T.astype(BF16)


def _attn_kernel(x_ref, q_ref, ks_ref, vs_ref, kb_ref, vbt_ref, kc_ref, vct_ref, wo_ref, mod_ref, g_ref,
                 b_ref, o_ref, att_ref):
    i = pl.program_id(0)
    tm = TM_SEQ
    is_prompt = i < N_P // tm

    def head(hq):
        return slice(hq * HEAD_DIM, (hq + 1) * HEAD_DIM)

    @pl.when(is_prompt)
    def _():
        for hk in range(N_KV_HEADS):
            q4 = jnp.concatenate([q_ref[:, head(hk * Q_PER_KV + g)] for g in range(Q_PER_KV)], axis=0)
            s = _dot_nt(q4, ks_ref[:, head(hk)])
            p = jnp.exp2(s - jnp.max(s, axis=-1, keepdims=True))
            o4 = _dot(p.astype(BF16), vs_ref[:, head(hk)]) / jnp.sum(p, axis=-1, keepdims=True)
            for g in range(Q_PER_KV):
                att_ref[:, head(hk * Q_PER_KV + g)] = o4[g * tm:(g + 1) * tm].astype(BF16)

    @pl.when(jnp.logical_not(is_prompt))
    def _():
        for hk in range(N_KV_HEADS):
            for g0 in range(0, Q_PER_KV, ATTN_PAIR):
                hq0 = hk * Q_PER_KV + g0
                qq = jnp.concatenate([q_ref[:, head(hq0 + g)] for g in range(ATTN_PAIR)], axis=0)
                st_b = _dot_nt(kb_ref[:, head(hk)], qq)
                st_c = _dot_nt(kc_ref[:, head(hk)], qq)
                mx = jnp.maximum(jnp.max(st_b, axis=0, keepdims=True), jnp.max(st_c, axis=0, keepdims=True))
                pt_b = jnp.exp2(st_b - mx)
                pt_c = jnp.exp2(st_c - mx)
                den = jnp.sum(pt_b, axis=0, keepdims=True) + jnp.sum(pt_c, axis=0, keepdims=True)
                ot = (_dot(vbt_ref[head(hk), :], pt_b.astype(BF16))
                      + _dot(vct_ref[head(hk), :], pt_c.astype(BF16))) / den
                oo = ot.T
                for g in range(ATTN_PAIR):
                    att_ref[:, head(hq0 + g)] = oo[g * tm:(g + 1) * tm].astype(BF16)

    y = _dot(att_ref[...], wo_ref[...])
    o_ref[...] = _post_norm(x_ref[...], y, mod_ref[...][2:3], g_ref[...], b_ref[...])


def _attention(x, mod, ln_g, ln_b, w_qkv, q_norm, k_norm, w_o, cache_k, cache_v, layer, j):
    tm = TM_QKV
    n_pt = N_P // tm
    cos, sin = _rope_tables(tm)
    rope_spec = pl.BlockSpec(
        (tm, HEAD_DIM),
        lambda i: (jnp.where(i < n_pt, DEC_SEQ // tm, jnp.maximum(i - n_pt, 0) % (DEC_SEQ // tm)), 0))
    prompt_row = _stream_specs(tm, KV_W)[0]
    latent_col = pl.BlockSpec((KV_W, tm), lambda i: (0, jnp.maximum(i - n_pt, 0)))
    q_gain = q_norm[j] * (math.log2(math.e) * HEAD_DIM ** -0.5)
    q, k, v, vt, new_k, new_v = pl.pallas_call(
        functools.partial(_qkv_kernel, tm=tm),
        grid=(N_TOK // tm,),
        in_specs=[
            _row_spec(tm),
            _mod_spec(tm, layer),
            _layer((D_MODEL, QKV_W), j),
            _resident((1, HEAD_DIM)), _resident((1, HEAD_DIM)),
            rope_spec, rope_spec,
        ],
        out_specs=[_row_spec(tm, Q_W), _row_spec(tm, KV_W), _row_spec(tm, KV_W), latent_col,
                   prompt_row, prompt_row],
        out_shape=[
            jax.ShapeDtypeStruct((N_TOK, Q_W), BF16),
            jax.ShapeDtypeStruct((N_TOK, KV_W), BF16),
            jax.ShapeDtypeStruct((N_TOK, KV_W), BF16),
            jax.ShapeDtypeStruct((KV_W, N_S), BF16),
            jax.ShapeDtypeStruct((N_P, KV_W), F32),
            jax.ShapeDtypeStruct((N_P, KV_W), F32),
        ],
        compiler_params=_params(),
        name="qkv_rope",
    )(x, mod, w_qkv, q_gain.reshape(1, HEAD_DIM), k_norm[j].reshape(1, HEAD_DIM), jnp.asarray(cos),
      jnp.asarray(sin))

    tm = TM_SEQ
    n_pt = N_P // tm
    per_seq = DEC_SEQ // tm
    batch_of = lambda i: jnp.maximum(i - n_pt, 0) // per_seq
    small = _stream_specs(tm, KV_W)[0]
    k_big = pl.BlockSpec((DEC_SEQ, KV_W), lambda i: (N_P // DEC_SEQ + batch_of(i), 0))
    vt_big = pl.BlockSpec((KV_W, DEC_SEQ), lambda i: (0, batch_of(i)))
    kc_spec = pl.BlockSpec((None, PAST_LEN, KV_W), lambda i: (batch_of(i), 0, 0))
    vct_spec = pl.BlockSpec((None, KV_W, PAST_LEN), lambda i: (batch_of(i), 0, 0))
    kc = cache_k[:, j].reshape(DEC_BATCH, PAST_LEN, KV_W).astype(BF16)
    vct = cache_v[:, j].reshape(DEC_BATCH, PAST_LEN, KV_W).transpose(0, 2, 1).astype(BF16)
    x_new = pl.pallas_call(
        _attn_kernel,
        grid=(N_TOK // tm,),
        in_specs=[
            _row_spec(tm), _row_spec(tm, Q_W),
            small, small, k_big, vt_big, kc_spec, vct_spec,
            _layer((Q_W, D_MODEL), j),
            _mod_spec(tm, layer),
            _ln_spec(layer, 0), _ln_spec(layer, 0),
        ],
        out_specs=_row_spec(tm),
        out_shape=jax.ShapeDtypeStruct((N_TOK, D_MODEL), F32),
        scratch_shapes=[pltpu.VMEM((tm, Q_W), BF16)],
        compiler_params=_params(),
        name="attention",
    )(x, q, k, v, k, vt, kc, vct, w_o, mod, ln_g, ln_b)
    return x_new, new_k, new_v


POOL_HALO = 8


def _pool_kernel(x_ref, xp_ref, xn_ref, mod_ref, w_ref, sc_ref, g_ref, b_ref, o_ref, buf_ref, p_ref):
    tm, halo = TM_SEQ, POOL_HALO
    start = pl.program_id(0) * tm
    length, pos0 = _seq_pos(start)
    m = mod_ref[...]
    shift, scale, gate = m[0:1], m[1:2], m[2:3]

    def modulate(v):
        return _ln(v) * (1.0 + scale) + shift

    x = x_ref[...]
    buf_ref[0:halo] = jnp.where(pos0 != 0, modulate(xp_ref[...]), 0.0)
    buf_ref[halo:halo + tm] = modulate(x)
    buf_ref[halo + tm:] = jnp.where(pos0 + tm != length, modulate(xn_ref[...]), 0.0)

    t = pos0 + lax.broadcasted_iota(jnp.int32, (tm, 1), 0)
    n = tm + 2 * halo
    for g, half in enumerate(POOL_HALF):
        cols = slice(g * GROUP_W, (g + 1) * GROUP_W)
        hbuf = buf_ref[:, cols]
        wsum = hbuf + pltpu.roll(hbuf, 1, 0)
        h = 1
        while h < half:
            wsum = pltpu.roll(wsum, h, 0) + pltpu.roll(wsum, n - h, 0)
            h *= 2
        cnt = (jnp.minimum(t + half, length) - jnp.maximum(t - half, 0)).astype(F32)
        pooled = wsum[halo:halo + tm] / cnt - hbuf[halo:halo + tm]
        p_ref[:, cols] = _dot(pooled.astype(BF16), w_ref[g])
    y = p_ref[...] * sc_ref[...]
    o_ref[...] = _post_norm(x, y, gate, g_ref[...], b_ref[...])


def _pool(x, mod, ln_g, ln_b, w_grp, scale, layer, j):
    tm, halo = TM_SEQ, POOL_HALO
    nb = N_TOK // halo
    return pl.pallas_call(
        _pool_kernel,
        grid=(N_TOK // tm,),
        in_specs=[
            _row_spec(tm),
            pl.BlockSpec((halo, D_MODEL), lambda i: (jnp.maximum(i * (tm // halo) - 1, 0), 0)),
            pl.BlockSpec((halo, D_MODEL), lambda i: (jnp.minimum((i + 1) * (tm // halo), nb - 1), 0)),
            _mod_spec(tm, layer),
            _layer((N_GROUPS, GROUP_W, GROUP_W), j),
            _layer((1, D_MODEL), j),
            _ln_spec(layer, 0), _ln_spec(layer, 0),
        ],
        out_specs=_row_spec(tm),
        out_shape=jax.ShapeDtypeStruct((N_TOK, D_MODEL), F32),
        scratch_shapes=[
            pltpu.VMEM((tm + 2 * halo, D_MODEL), F32),
            pltpu.VMEM((tm, D_MODEL), F32),
        ],
        compiler_params=_params(),
        name="pool_mixer",
    )(x, x, x, mod, w_grp, scale, ln_g, ln_b)


def _gmlp_kernel(x_ref, mod_ref, win_ref, bin_ref, vg_ref, vb_ref, ws_ref, bs_ref, wo_ref, g_ref, b_ref,
                 o_ref, act_ref, v_ref, *, tm):
    m = mod_ref[...]
    x = x_ref[...]
    h = (_ln(x) * (1.0 + m[1:2]) + m[0:1]).astype(BF16)

    def branch(lo, width):
        return jax.nn.gelu(_dot(h, win_ref[:, lo:lo + width]) + bin_ref[:, lo:lo + width])

    half = GMLP_W // 2
    v = jnp.concatenate([branch(GMLP_W, half), branch(GMLP_W + half, half)], axis=1)
    v_ref[...] = (_ln(v) * vg_ref[...] + vb_ref[...]).astype(BF16)
    bs = bs_ref[...]
    groups_per_half = half // GMLP_GW
    for uc in range(GMLP_W // half):
        u = branch(uc * half, half)
        for gl in range(groups_per_half):
            g = uc * groups_per_half + gl
            cols = slice(g * GMLP_GW, (g + 1) * GMLP_GW)
            for n in range(tm // CHUNK):
                rows = slice(n * CHUNK, (n + 1) * CHUNK)
                s = _dot(ws_ref[g], v_ref[rows, cols]) + bs[:, g:g + 1]
                act_ref[rows, cols] = (u[rows, gl * GMLP_GW:(gl + 1) * GMLP_GW] * s).astype(BF16)
    y = _dot(act_ref[...], wo_ref[...])
    o_ref[...] = _post_norm(x, y, m[2:3], g_ref[...], b_ref[...])


def _gmlp(x, mod, ln_g, ln_b, w_in, b_in, v_g, v_b, w_s, b_s_t, w_o, layer, j):
    tm = TM_GMLP
    return pl.pallas_call(
        functools.partial(_gmlp_kernel, tm=tm),
        grid=(N_TOK // tm,),
        in_specs=[
            _row_spec(tm),
            _mod_spec(tm, layer),
            _layer((D_MODEL, 2 * GMLP_W), j),
            _layer((1, 2 * GMLP_W), j),
            _layer((1, GMLP_W), j), _layer((1, GMLP_W), j),
            _layer((GMLP_GROUPS, CHUNK, CHUNK), j),
            _layer((CHUNK, GMLP_GROUPS), j),
            _layer((GMLP_W, D_MODEL), j),
            _ln_spec(layer, 0), _ln_spec(layer, 0),
        ],
        out_specs=_row_spec(tm),
        out_shape=jax.ShapeDtypeStruct((N_TOK, D_MODEL), F32),
        scratch_shapes=[pltpu.VMEM((tm, GMLP_W), BF16)] * 2,
        compiler_params=_params(),
        name="gmlp_mixer",
    )(x, mod, w_in, b_in, v_g, v_b, w_s, b_s_t, w_o, ln_g, ln_b)


def kernel(x_prompt, x_sample, cache_k, cache_v, c, c_ctx, w_mod, b_mod, ln_g, ln_b, ffn_w_up, ffn_conv_w,
           ffn_conv_b, ffn_w_down, fnet_w_o, attn_w_qkv, attn_q_norm, attn_k_norm, attn_w_o, pool_w,
           pool_scale, gmlp_w_in, gmlp_b_in, gmlp_ln_g, gmlp_ln_b, gmlp_w_s, gmlp_b_s, gmlp_w_o):
    assert DEPTH == 4
    cond = jnp.concatenate([c_ctx[None, :], c, jnp.zeros((COND_PAD - N_COND, D_MODEL), F32)], axis=0)
    mod = _modulation(cond, w_mod, b_mod)
    ln_g = ln_g.reshape(DEPTH * 2, 1, D_MODEL)
    ln_b = ln_b.reshape(DEPTH * 2, 1, D_MODEL)
    ffn = (ffn_w_up.astype(BF16), ffn_conv_w, ffn_conv_b.reshape(DEPTH, 1, D_FF), ffn_w_down.astype(BF16))

    x = _fnet(x_prompt.reshape(N_P, D_MODEL), x_sample.reshape(N_S, D_MODEL), mod, ln_g, ln_b,
              fnet_w_o.astype(BF16), 0, 0)
    x = _ffn(x, mod, ln_g, ln_b, *ffn, 0, False)
    x, new_k, new_v = _attention(x, mod, ln_g, ln_b, attn_w_qkv.astype(BF16), attn_q_norm, attn_k_norm,
                                 attn_w_o.astype(BF16), cache_k, cache_v, 1, 0)
    x = _ffn(x, mod, ln_g, ln_b, *ffn, 1, False)
    x = _pool(x, mod, ln_g, ln_b, pool_w.astype(BF16), pool_scale.reshape(-1, 1, D_MODEL), 2, 0)
    x = _ffn(x, mod, ln_g, ln_b, *ffn, 2, False)
    x = _gmlp(x, mod, ln_g, ln_b, gmlp_w_in.astype(BF16), gmlp_b_in.reshape(-1, 1, 2 * GMLP_W),
              gmlp_ln_g.reshape(-1, 1, GMLP_W), gmlp_ln_b.reshape(-1, 1, GMLP_W), gmlp_w_s.astype(BF16),
              gmlp_b_s.transpose(0, 2, 1), gmlp_w_o.astype(BF16), 3, 0)
    y_prompt, y_sample = _ffn(x, mod, ln_g, ln_b, *ffn, 3, True)

    return (y_prompt.reshape(BATCH, SEQ, D_MODEL), y_sample.reshape(DEC_BATCH, DEC_SEQ, D_MODEL),
            new_k.reshape(BATCH, 1, SEQ, N_KV_HEADS, HEAD_DIM), new_v.reshape(BATCH, 1, SEQ, N_KV_HEADS, HEAD_DIM))
```

```python
import functools
import math

import numpy as np
import jax
import jax.numpy as jnp
from jax import lax
from jax.experimental import pallas as pl
from jax.experimental.pallas import tpu as pltpu

D_MODEL = 1024
BATCH = 32
SEQ = 256
DEPTH = 4
DEC_BATCH = 2
DEC_SEQ = 4096
PAST_LEN = 512
GRID_W = 64
N_GROUPS = 4
GROUP_W = D_MODEL // N_GROUPS
HEAD_DIM = 128
N_Q_HEADS = D_MODEL // HEAD_DIM
N_KV_HEADS = 2
Q_PER_KV = N_Q_HEADS // N_KV_HEADS
Q_W = N_Q_HEADS * HEAD_DIM
KV_W = N_KV_HEADS * HEAD_DIM
QKV_W = Q_W + 2 * KV_W
ROPE_THETA = 10000.0
POOL_HALF = (1, 2, 4, 8)
CHUNK = 128
GMLP_W = 2 * D_MODEL
GMLP_GROUPS = 4
GMLP_GW = GMLP_W // GMLP_GROUPS
D_FF = ((8 * D_MODEL // 3 + 127) // 128) * 128
ALPHA = (2 * DEPTH) ** 0.25
LN_EPS = 1e-6

N_P = BATCH * SEQ
N_S = DEC_BATCH * DEC_SEQ
N_TOK = N_P + N_S
N_COND = 1 + DEC_BATCH
COND_PAD = 8

F32 = jnp.float32
BF16 = jnp.bfloat16

VMEM_LIMIT_BYTES = 56 * 1024 * 1024
BF16_ROWS = 16

TM_SEQ = SEQ
TM_FFN = 512
TM_QKV = 512
TM_GMLP = 512
LANES = 128
MXU_DIM = 256
FFN_CHUNKS = (5 * MXU_DIM, 6 * MXU_DIM)
assert sum(FFN_CHUNKS) == D_FF
DFT_RADIX = 8
DFT_SUB = DEC_SEQ // DFT_RADIX
DFT_COEF_ROWS = 128
DFT_CH = 512
ATTN_PAIR = 2


def _dot(a, b):
    return jnp.dot(a, b, preferred_element_type=F32)


def _dot_nt(a, b):
    return lax.dot_general(a, b, (((1,), (1,)), ((), ())), preferred_element_type=F32)


def _ln(x):
    mu = jnp.mean(x, axis=-1, keepdims=True)
    xc = x - mu
    var = jnp.mean(xc * xc, axis=-1, keepdims=True)
    return xc * lax.rsqrt(var + LN_EPS)


def _post_norm(x, y, gate, g, b):
    return _ln(ALPHA * x + gate * y) * g + b


def _cond_row(row_start):
    return jnp.maximum(row_start - (N_P - DEC_SEQ), 0) // DEC_SEQ


def _seq_pos(row_start):
    is_p = row_start < N_P
    length = jnp.where(is_p, SEQ, DEC_SEQ)
    pos = jnp.where(is_p, row_start, row_start - N_P) & (length - 1)
    return length, pos


def _params():
    return pltpu.CompilerParams(dimension_semantics=("arbitrary",), vmem_limit_bytes=VMEM_LIMIT_BYTES)


def _resident(shape):
    return pl.BlockSpec(shape, lambda i: (0,) * len(shape), pipeline_mode=pl.Buffered(1))


def _layer(tail, layer):
    return pl.BlockSpec((None,) + tuple(tail), lambda i: (layer,) + (0,) * len(tail),
                        pipeline_mode=pl.Buffered(1))


def _mod_spec(tm, layer):
    return pl.BlockSpec((None, None, 6, D_MODEL), lambda i: (layer, _cond_row(i * tm), 0, 0))


def _ln_spec(layer, which):
    return pl.BlockSpec((None, 1, D_MODEL), lambda i: (2 * layer + which, 0, 0),
                        pipeline_mode=pl.Buffered(1))


def _row_spec(tm, width=D_MODEL):
    return pl.BlockSpec((tm, width), lambda i: (i, 0))


def _stream_specs(tm, width=D_MODEL):
    n_pt = N_P // tm
    return (pl.BlockSpec((tm, width), lambda i: (jnp.minimum(i, n_pt - 1), 0)),
            pl.BlockSpec((tm, width), lambda i: (jnp.maximum(i - n_pt, 0), 0)))


CAST_STEPS = 16


def _cast_plan(w, first_layer, n_layers, first_step):
    _, rows, width = w.shape
    rb = rows // CAST_STEPS
    assert rb * CAST_STEPS == rows and rb % BF16_ROWS == 0

    def block(i):
        j = jnp.clip(i - first_step, 0, n_layers * CAST_STEPS - 1)
        return j // CAST_STEPS, j % CAST_STEPS

    in_spec = pl.BlockSpec((None, rb, width), lambda i: (first_layer + block(i)[0], block(i)[1], 0))
    out_spec = pl.BlockSpec((None, rb, width), lambda i: (block(i)[0], block(i)[1], 0))
    return w, in_spec, out_spec, jax.ShapeDtypeStruct((n_layers, rows, width), BF16)


def _run_casts(src_refs, dst_refs):
    for src, dst in zip(src_refs, dst_refs, strict=True):
        dst[...] = src[...].astype(BF16)


def _mod_kernel(c_ref, w_ref, b_ref, o_ref):
    c = c_ref[...]
    s = c / (1.0 + jnp.exp(-c))
    o_ref[...] = _dot(s.astype(BF16), w_ref[...].astype(BF16)) + b_ref[...]


def _modulation(cond, w_mod, b_mod):
    tn = 2048
    out = pl.pallas_call(
        _mod_kernel,
        grid=(DEPTH, 6 * D_MODEL // tn),
        in_specs=[
            pl.BlockSpec((COND_PAD, D_MODEL), lambda l, j: (0, 0)),
            pl.BlockSpec((None, D_MODEL, tn), lambda l, j: (l, 0, j)),
            pl.BlockSpec((None, 1, tn), lambda l, j: (l, 0, j)),
        ],
        out_specs=pl.BlockSpec((None, COND_PAD, tn), lambda l, j: (l, 0, j)),
        out_shape=jax.ShapeDtypeStruct((DEPTH, COND_PAD, 6 * D_MODEL), F32),
        compiler_params=pltpu.CompilerParams(
            dimension_semantics=("arbitrary", "arbitrary"), vmem_limit_bytes=VMEM_LIMIT_BYTES),
        name="modulation",
    )(cond, w_mod, b_mod.reshape(DEPTH, 1, 6 * D_MODEL))
    return out[:, :N_COND].reshape(DEPTH, N_COND, 6, D_MODEL)


def _ffn_kernel(x_ref, xp_ref, xn_ref, mod_ref, g_ref, b_ref, wup_ref, cw_ref, cb_ref, wdn_ref,
                *rest, tm, split_out):
    halo = BF16_ROWS
    out_refs, (hext_ref, aext_ref) = rest[:-2], rest[-2:]
    start = pl.program_id(0) * tm
    length, pos0 = _seq_pos(start)
    m = mod_ref[...]
    shift, scale, gate = m[3:4], m[4:5], m[5:6]

    def modulate(v):
        return (_ln(v) * (1.0 + scale) + shift).astype(BF16)

    x = x_ref[...]
    hext_ref[0:halo] = modulate(xp_ref[...])
    hext_ref[halo:halo + tm] = modulate(x)
    hext_ref[halo + tm:] = modulate(xn_ref[...])

    t = (pos0 + lax.broadcasted_iota(jnp.int32, (tm, 1), 0)) & (length - 1)
    has_prev = t != 0
    has_next = t != length - 1

    acc = jnp.zeros((tm, D_MODEL), F32)
    lo = 0
    for fc in FFN_CHUNKS:
        aext_ref[:, 0:fc] = _dot(hext_ref[...], wup_ref[:, lo:lo + fc])
        bgate = _dot(hext_ref[halo:halo + tm], wup_ref[:, D_FF + lo:D_FF + lo + fc])
        cw = cw_ref[:, lo:lo + fc]
        a = (jnp.where(has_prev, aext_ref[halo - 1:halo - 1 + tm, 0:fc], 0.0) * cw[0:1]
             + aext_ref[halo:halo + tm, 0:fc] * cw[1:2]
             + jnp.where(has_next, aext_ref[halo + 1:halo + 1 + tm, 0:fc], 0.0) * cw[2:3]
             + cb_ref[:, lo:lo + fc])
        act = (jax.nn.gelu(a) * bgate).astype(BF16)
        acc = acc + _dot(act, wdn_ref[lo:lo + fc, :])
        lo += fc
    y = _post_norm(x, acc, gate, g_ref[...], b_ref[...])

    if split_out:
        @pl.when(start < N_P)
        def _():
            out_refs[0][...] = y

        @pl.when(start >= N_P)
        def _():
            out_refs[1][...] = y
    else:
        out_refs[0][...] = y


def _ffn(x, mod, ln_g, ln_b, w_up, conv_w, conv_b, w_down, layer, w_layer, split_out):
    tm, halo = TM_FFN, BF16_ROWS
    nb = N_TOK // halo
    if split_out:
        out_specs = list(_stream_specs(tm))
        out_shape = [jax.ShapeDtypeStruct((N_P, D_MODEL), F32), jax.ShapeDtypeStruct((N_S, D_MODEL), F32)]
    else:
        out_specs = _row_spec(tm)
        out_shape = jax.ShapeDtypeStruct((N_TOK, D_MODEL), F32)
    return pl.pallas_call(
        functools.partial(_ffn_kernel, tm=tm, split_out=split_out),
        grid=(N_TOK // tm,),
        in_specs=[
            _row_spec(tm),
            pl.BlockSpec((halo, D_MODEL), lambda i: (jnp.maximum(i * (tm // halo) - 1, 0), 0)),
            pl.BlockSpec((halo, D_MODEL), lambda i: (jnp.minimum((i + 1) * (tm // halo), nb - 1), 0)),
            _mod_spec(tm, layer),
            _ln_spec(layer, 1), _ln_spec(layer, 1),
            _layer((D_MODEL, 2 * D_FF), w_layer),
            _layer((3, D_FF), layer),
            _layer((1, D_FF), layer),
            _layer((D_FF, D_MODEL), w_layer),
        ],
        out_specs=out_specs,
        out_shape=out_shape,
        scratch_shapes=[
            pltpu.VMEM((tm + 2 * halo, D_MODEL), BF16),
            pltpu.VMEM((tm + 2 * halo, max(FFN_CHUNKS)), F32),
        ],
        compiler_params=_params(),
        name="conv_ffn",
    )(x, x, x, mod, ln_g, ln_b, w_up, conv_w, conv_b, w_down)


def _dft_matrix(n):
    k = np.arange(n, dtype=np.int64)
    ang = 2.0 * np.pi * ((k[:, None] * k[None, :]) % n) / n
    return np.cos(ang).astype(np.float32), np.sin(ang).astype(np.float32)


def _dft_twiddle_tables():
    t1 = np.arange(DFT_RADIX, dtype=np.int64)[:, None]
    k2 = np.arange(DFT_SUB, dtype=np.int64)[None, :]
    ang = 2.0 * np.pi * (t1 * k2) / DEC_SEQ
    shape = (DFT_RADIX, DFT_SUB // DFT_COEF_ROWS, DFT_COEF_ROWS)
    pad = ((0, 0), (0, DFT_COEF_ROWS - shape[1]), (0, 0))
    f = lambda a: np.pad(a.reshape(shape), pad).astype(np.float32)
    return f(np.cos(ang)), f(np.sin(ang))


def _octant(n):
    r = math.sqrt(0.5)
    return [(1.0, 0.0), (r, r), (0.0, 1.0), (-r, r), (-1.0, 0.0), (-r, -r), (0.0, -1.0), (r, -r)][n % 8]


def _fnet_a_kernel(xp_ref, xs_ref, mod_ref, cs_ref, pp_ref, qp_ref, pd_ref, qd_ref, p_scr, q_scr, *, tm):
    is_prompt = pl.program_id(0) < N_P // tm
    x = jnp.where(is_prompt, xp_ref[...], xs_ref[...])
    m = mod_ref[...]
    h = (_ln(x) * (1.0 + m[1:2]) + m[0:1]).astype(BF16)
    per_group = GROUP_W // LANES
    for g in range(N_GROUPS):
        r = _dot(h[:, g * GROUP_W:(g + 1) * GROUP_W], cs_ref[...])
        for c in range(per_group):
            p_scr[g * per_group + c] = r[:, c * LANES:(c + 1) * LANES]
            q_scr[g * per_group + c] = r[:, GROUP_W + c * LANES:GROUP_W + (c + 1) * LANES]

    @pl.when(is_prompt)
    def _():
        for c in range(D_MODEL // LANES):
            pp_ref[:, c * LANES:(c + 1) * LANES] = p_scr[c].astype(BF16)
            qp_ref[:, c * LANES:(c + 1) * LANES] = q_scr[c].astype(BF16)

    @pl.when(jnp.logical_not(is_prompt))
    def _():
        for t1 in range(DFT_RADIX):
            rows = pl.ds(t1, tm // DFT_RADIX, stride=DFT_RADIX)
            for c in range(D_MODEL // LANES):
                pd_ref[t1, :, c * LANES:(c + 1) * LANES] = p_scr[c, rows, :].astype(BF16)
                qd_ref[t1, :, c * LANES:(c + 1) * LANES] = q_scr[c, rows, :].astype(BF16)


def _fnet_latent_kernel(pd_ref, qd_ref, cs_ref, ct_ref, st_ref, f_ref, a_ref, b_ref):
    for t1 in range(DFT_RADIX):
        csp = _dot(cs_ref[...], pd_ref[t1])
        csq = _dot(cs_ref[...], qd_ref[t1])
        u = csp[:DFT_SUB] - csq[DFT_SUB:]
        v = csq[:DFT_SUB] + csp[DFT_SUB:]
        if t1 == 0:
            a_ref[0], b_ref[0] = u, v
            continue
        ct, st = ct_ref[t1].T, st_ref[t1].T
        for blk in range(DFT_SUB // DFT_COEF_ROWS):
            rows = slice(blk * DFT_COEF_ROWS, (blk + 1) * DFT_COEF_ROWS)
            cc, ss = ct[:, blk:blk + 1], st[:, blk:blk + 1]
            a_ref[t1, rows, :] = cc * u[rows] - ss * v[rows]
            b_ref[t1, rows, :] = ss * u[rows] + cc * v[rows]

    norm = (DEC_SEQ * GROUP_W) ** -0.5
    for k1 in range(DFT_RADIX):
        for blk in range(DFT_SUB // DFT_COEF_ROWS):
            rows = slice(blk * DFT_COEF_ROWS, (blk + 1) * DFT_COEF_ROWS)
            acc = a_ref[0, rows, :]
            for t1 in range(1, DFT_RADIX):
                c, s = _octant(k1 * t1)
                if s == 0.0:
                    acc = acc + a_ref[t1, rows, :] if c > 0 else acc - a_ref[t1, rows, :]
                elif c == 0.0:
                    acc = acc - b_ref[t1, rows, :] if s > 0 else acc + b_ref[t1, rows, :]
                else:
                    acc = acc + (c * a_ref[t1, rows, :] - s * b_ref[t1, rows, :])
            out_rows = slice(k1 * DFT_SUB + blk * DFT_COEF_ROWS, k1 * DFT_SUB + (blk + 1) * DFT_COEF_ROWS)
            f_ref[out_rows, :] = (acc * norm).astype(BF16)


def _fnet_b_kernel(xp_ref, xs_ref, pp_ref, qp_ref, fs_ref, c256_ref, s256_ref, wo_ref, mod_ref, g_ref,
                   b_ref, *rest, tm, n_cast):
    cast_src, o_ref, cast_dst, f_ref = rest[:n_cast], rest[n_cast], rest[n_cast + 1:-1], rest[-1]
    _run_casts(cast_src, cast_dst)
    is_prompt = pl.program_id(0) < N_P // tm

    @pl.when(is_prompt)
    def _():
        for s in range(tm // SEQ):
            rows = slice(s * SEQ, (s + 1) * SEQ)
            f = _dot(c256_ref[...], pp_ref[rows, :]) - _dot(s256_ref[...], qp_ref[rows, :])
            f_ref[rows, :] = (f * (SEQ * GROUP_W) ** -0.5).astype(BF16)

    @pl.when(jnp.logical_not(is_prompt))
    def _():
        f_ref[...] = fs_ref[...]

    x = jnp.where(is_prompt, xp_ref[...], xs_ref[...])
    y = _dot(f_ref[...], wo_ref[...])
    o_ref[...] = _post_norm(x, y, mod_ref[...][2:3], g_ref[...], b_ref[...])


def _fnet(xp, xs, mod, ln_g, ln_b, w_o, layer, j, casts):
    c256, s256 = _dft_matrix(SEQ)
    cs = jnp.concatenate([jnp.asarray(c256), jnp.asarray(s256)], axis=1).astype(BF16)
    tm = 512
    n_pt = N_P // tm
    per_seq = DEC_SEQ // tm
    prompt_rows = _stream_specs(tm)[0]
    slabs = pl.BlockSpec((None, DFT_RADIX, tm // DFT_RADIX, D_MODEL),
                         lambda i: (jnp.maximum(i - n_pt, 0) // per_seq, 0, jnp.maximum(i - n_pt, 0) % per_seq, 0))
    dec_shape = jax.ShapeDtypeStruct((DEC_BATCH, DFT_RADIX, DFT_SUB, D_MODEL), BF16)
    pp, qp, pd, qd = pl.pallas_call(
        functools.partial(_fnet_a_kernel, tm=tm),
        grid=(N_TOK // tm,),
        in_specs=[*_stream_specs(tm), _mod_spec(tm, layer), _resident((GROUP_W, 2 * GROUP_W))],
        out_specs=[prompt_rows, prompt_rows, slabs, slabs],
        out_shape=[jax.ShapeDtypeStruct((N_P, D_MODEL), BF16)] * 2 + [dec_shape] * 2,
        scratch_shapes=[pltpu.VMEM((D_MODEL // LANES, tm, LANES), F32)] * 2,
        compiler_params=_params(),
        name="fnet_channel_dft",
    )(xp, xs, mod, cs)

    c_sub, s_sub = _dft_matrix(DFT_SUB)
    cs_sub = jnp.concatenate([jnp.asarray(c_sub), jnp.asarray(s_sub)], axis=0).astype(BF16)
    ct, st = _dft_twiddle_tables()
    slabs_in = pl.BlockSpec((None, DFT_RADIX, DFT_SUB, DFT_CH), lambda b, c: (b, 0, 0, c))
    whole = lambda shape: pl.BlockSpec(shape, lambda b, c: (0,) * len(shape), pipeline_mode=pl.Buffered(1))
    f_s = pl.pallas_call(
        _fnet_latent_kernel,
        grid=(DEC_BATCH, D_MODEL // DFT_CH),
        in_specs=[slabs_in, slabs_in, whole((2 * DFT_SUB, DFT_SUB)), whole(ct.shape), whole(st.shape)],
        out_specs=pl.BlockSpec((DEC_SEQ, DFT_CH), lambda b, c: (b, c)),
        out_shape=jax.ShapeDtypeStruct((N_S, D_MODEL), BF16),
        scratch_shapes=[pltpu.VMEM((DFT_RADIX, DFT_SUB, DFT_CH), F32)] * 2,
        compiler_params=pltpu.CompilerParams(
            dimension_semantics=("arbitrary",) * 2, vmem_limit_bytes=VMEM_LIMIT_BYTES),
        name="fnet_latent_dft",
    )(pd, qd, cs_sub, jnp.asarray(ct), jnp.asarray(st))

    prompt_rows, latent_rows = _stream_specs(tm)
    return pl.pallas_call(
        functools.partial(_fnet_b_kernel, tm=tm, n_cast=len(casts)),
        grid=(N_TOK // tm,),
        in_specs=[
            *_stream_specs(tm),
            prompt_rows, prompt_rows, latent_rows,
            _resident((SEQ, SEQ)), _resident((SEQ, SEQ)),
            _layer((D_MODEL, D_MODEL), j),
            _mod_spec(tm, layer),
            _ln_spec(layer, 0), _ln_spec(layer, 0),
            *[c[1] for c in casts],
        ],
        out_specs=[_row_spec(tm), *[c[2] for c in casts]],
        out_shape=[jax.ShapeDtypeStruct((N_TOK, D_MODEL), F32), *[c[3] for c in casts]],
        scratch_shapes=[pltpu.VMEM((tm, D_MODEL), BF16)],
        compiler_params=_params(),
        name="fnet_token_dft",
    )(xp, xs, pp, qp, f_s, jnp.asarray(c256).astype(BF16), jnp.asarray(s256).astype(BF16), w_o, mod,
      ln_g, ln_b, *[c[0] for c in casts])


def _rope_tables(tm):
    quarter = HEAD_DIM // 4
    t = np.arange(DEC_SEQ)
    row, col = (t // GRID_W).astype(np.float32), (t % GRID_W).astype(np.float32)
    inv = (np.float32(ROPE_THETA) ** (-np.arange(quarter, dtype=np.float32) / np.float32(quarter))).astype(np.float32)
    ang_r = (row[:, None] * inv[None, :]).astype(np.float32).astype(np.float64)
    ang_c = (col[:, None] * inv[None, :]).astype(np.float32).astype(np.float64)
    cos = np.concatenate([np.cos(ang_r), np.cos(ang_r), np.cos(ang_c), np.cos(ang_c)], axis=1)
    sin = np.concatenate([-np.sin(ang_r), np.sin(ang_r), -np.sin(ang_c), np.sin(ang_c)], axis=1)
    cos = np.concatenate([cos, np.ones((tm, HEAD_DIM))], axis=0)
    sin = np.concatenate([sin, np.zeros((tm, HEAD_DIM))], axis=0)
    return np.asarray(cos, np.float32), np.asarray(sin, np.float32)


def _qkv_kernel(x_ref, mod_ref, w_ref, qg_ref, kg_ref, cos_ref, sin_ref,
                q_ref, k_ref, v_ref, vt_ref, nk_ref, nv_ref, *, tm):
    i = pl.program_id(0)
    m = mod_ref[...]
    h = (_ln(x_ref[...]) * (1.0 + m[1:2]) + m[0:1]).astype(BF16)
    cos, sin = cos_ref[...], sin_ref[...]
    lane = lax.broadcasted_iota(jnp.int32, (1, HEAD_DIM), 1)
    low = (lane % (HEAD_DIM // 2)) < (HEAD_DIM // 4)

    def rms(v, g):
        return v * lax.rsqrt(jnp.mean(v * v, axis=-1, keepdims=True) + LN_EPS) * g

    def rope(v):
        partner = jnp.where(low, pltpu.roll(v, HEAD_DIM - HEAD_DIM // 4, 1), pltpu.roll(v, HEAD_DIM // 4, 1))
        return v * cos + partner * sin

    def project(lo):
        return _dot(h, w_ref[:, lo:lo + MXU_DIM])

    heads_per_block = MXU_DIM // HEAD_DIM
    for blk in range(Q_W // MXU_DIM):
        qq = project(blk * MXU_DIM)
        for j in range(heads_per_block):
            sl = slice((blk * heads_per_block + j) * HEAD_DIM, (blk * heads_per_block + j + 1) * HEAD_DIM)
            q_ref[:, sl] = rope(rms(qq[:, j * HEAD_DIM:(j + 1) * HEAD_DIM], qg_ref[...])).astype(BF16)
    kk = project(Q_W)
    kn = [rms(kk[:, hk * HEAD_DIM:(hk + 1) * HEAD_DIM], kg_ref[...]) for hk in range(N_KV_HEADS)]
    for hk in range(N_KV_HEADS):
        k_ref[:, hk * HEAD_DIM:(hk + 1) * HEAD_DIM] = rope(kn[hk]).astype(BF16)
    v = project(Q_W + KV_W)
    v_ref[...] = v.astype(BF16)

    is_prompt = i < N_P // tm

    @pl.when(is_prompt)
    def _():
        for hk in range(N_KV_HEADS):
            nk_ref[:, hk * HEAD_DIM:(hk + 1) * HEAD_DIM] = kn[hk]
        nv_ref[...] = v

    @pl.when(jnp.logical_not(is_prompt))
    def _():
        vt_ref[...] = v.T.astype(BF16)


def _attn_kernel(x_ref, q_ref, ks_ref, vs_ref, kb_ref, vbt_ref, kc_ref, vct_ref, wo_ref, mod_ref, g_ref,
                 b_ref, *rest, n_cast):
    cast_src, o_ref, cast_dst, att_ref = rest[:n_cast], rest[n_cast], rest[n_cast + 1:-1], rest[-1]
    _run_casts(cast_src, cast_dst)
    i = pl.program_id(0)
    tm = TM_SEQ
    is_prompt = i < N_P // tm

    def head(hq):
        return slice(hq * HEAD_DIM, (hq + 1) * HEAD_DIM)

    @pl.when(is_prompt)
    def _():
        for hk in range(N_KV_HEADS):
            q4 = jnp.concatenate([q_ref[:, head(hk * Q_PER_KV + g)] for g in range(Q_PER_KV)], axis=0)
            s = _dot_nt(q4, ks_ref[:, head(hk)])
            p = jnp.exp2(s - jnp.max(s, axis=-1, keepdims=True))
            o4 = _dot(p.astype(BF16), vs_ref[:, head(hk)]) / jnp.sum(p, axis=-1, keepdims=True)
            for g in range(Q_PER_KV):
                att_ref[:, head(hk * Q_PER_KV + g)] = o4[g * tm:(g + 1) * tm].astype(BF16)

    @pl.when(jnp.logical_not(is_prompt))
    def _():
        for hk in range(N_KV_HEADS):
            for g0 in range(0, Q_PER_KV, ATTN_PAIR):
                hq0 = hk * Q_PER_KV + g0
                qq = jnp.concatenate([q_ref[:, head(hq0 + g)] for g in range(ATTN_PAIR)], axis=0)
                st_b = _dot_nt(kb_ref[:, head(hk)], qq)
                st_c = _dot_nt(kc_ref[:, head(hk)], qq)
                mx = jnp.maximum(jnp.max(st_b, axis=0, keepdims=True), jnp.max(st_c, axis=0, keepdims=True))
                pt_b = jnp.exp2(st_b - mx)
                pt_c = jnp.exp2(st_c - mx)
                den = jnp.sum(pt_b, axis=0, keepdims=True) + jnp.sum(pt_c, axis=0, keepdims=True)
                ot = (_dot(vbt_ref[head(hk), :], pt_b.astype(BF16))
                      + _dot(vct_ref[head(hk), :], pt_c.astype(BF16))) / den
                oo = ot.T
                for g in range(ATTN_PAIR):
                    att_ref[:, head(hq0 + g)] = oo[g * tm:(g + 1) * tm].astype(BF16)

    y = _dot(att_ref[...], wo_ref[...])
    o_ref[...] = _post_norm(x_ref[...], y, mod_ref[...][2:3], g_ref[...], b_ref[...])


def _attention(x, mod, ln_g, ln_b, w_qkv, q_norm, k_norm, w_o, cache_k, cache_v, layer, j, casts):
    tm = TM_QKV
    n_pt = N_P // tm
    cos, sin = _rope_tables(tm)
    rope_spec = pl.BlockSpec(
        (tm, HEAD_DIM),
        lambda i: (jnp.where(i < n_pt, DEC_SEQ // tm, jnp.maximum(i - n_pt, 0) % (DEC_SEQ // tm)), 0))
    prompt_row = _stream_specs(tm, KV_W)[0]
    latent_col = pl.BlockSpec((KV_W, tm), lambda i: (0, jnp.maximum(i - n_pt, 0)))
    q_gain = q_norm[j] * (math.log2(math.e) * HEAD_DIM ** -0.5)
    q, k, v, vt, new_k, new_v = pl.pallas_call(
        functools.partial(_qkv_kernel, tm=tm),
        grid=(N_TOK // tm,),
        in_specs=[
            _row_spec(tm),
            _mod_spec(tm, layer),
            _layer((D_MODEL, QKV_W), j),
            _resident((1, HEAD_DIM)), _resident((1, HEAD_DIM)),
            rope_spec, rope_spec,
        ],
        out_specs=[_row_spec(tm, Q_W), _row_spec(tm, KV_W), _row_spec(tm, KV_W), latent_col,
                   prompt_row, prompt_row],
        out_shape=[
            jax.ShapeDtypeStruct((N_TOK, Q_W), BF16),
            jax.ShapeDtypeStruct((N_TOK, KV_W), BF16),
            jax.ShapeDtypeStruct((N_TOK, KV_W), BF16),
            jax.ShapeDtypeStruct((KV_W, N_S), BF16),
            jax.ShapeDtypeStruct((N_P, KV_W), F32),
            jax.ShapeDtypeStruct((N_P, KV_W), F32),
        ],
        compiler_params=_params(),
        name="qkv_rope",
    )(x, mod, w_qkv, q_gain.reshape(1, HEAD_DIM), k_norm[j].reshape(1, HEAD_DIM), jnp.asarray(cos),
      jnp.asarray(sin))

    tm = TM_SEQ
    n_pt = N_P // tm
    per_seq = DEC_SEQ // tm
    batch_of = lambda i: jnp.maximum(i - n_pt, 0) // per_seq
    small = _stream_specs(tm, KV_W)[0]
    k_big = pl.BlockSpec((DEC_SEQ, KV_W), lambda i: (N_P // DEC_SEQ + batch_of(i), 0))
    vt_big = pl.BlockSpec((KV_W, DEC_SEQ), lambda i: (0, batch_of(i)))
    kc_spec = pl.BlockSpec((None, PAST_LEN, KV_W), lambda i: (batch_of(i), 0, 0))
    vct_spec = pl.BlockSpec((None, KV_W, PAST_LEN), lambda i: (batch_of(i), 0, 0))
    kc = cache_k[:, j].reshape(DEC_BATCH, PAST_LEN, KV_W).astype(BF16)
    vct = cache_v[:, j].reshape(DEC_BATCH, PAST_LEN, KV_W).transpose(0, 2, 1).astype(BF16)
    x_new, *cast_out = pl.pallas_call(
        functools.partial(_attn_kernel, n_cast=len(casts)),
        grid=(N_TOK // tm,),
        in_specs=[
            _row_spec(tm), _row_spec(tm, Q_W),
            small, small, k_big, vt_big, kc_spec, vct_spec,
            _layer((Q_W, D_MODEL), j),
            _mod_spec(tm, layer),
            _ln_spec(layer, 0), _ln_spec(layer, 0),
            *[c[1] for c in casts],
        ],
        out_specs=[_row_spec(tm), *[c[2] for c in casts]],
        out_shape=[jax.ShapeDtypeStruct((N_TOK, D_MODEL), F32), *[c[3] for c in casts]],
        scratch_shapes=[pltpu.VMEM((tm, Q_W), BF16)],
        compiler_params=_params(),
        name="attention",
    )(x, q, k, v, k, vt, kc, vct, w_o, mod, ln_g, ln_b, *[c[0] for c in casts])
    return (x_new, new_k, new_v, *cast_out)


POOL_HALO = 8


def _pool_kernel(x_ref, xp_ref, xn_ref, mod_ref, w_ref, sc_ref, g_ref, b_ref, o_ref, buf_ref, p_ref):
    tm, halo = TM_SEQ, POOL_HALO
    start = pl.program_id(0) * tm
    length, pos0 = _seq_pos(start)
    m = mod_ref[...]
    shift, scale, gate = m[0:1], m[1:2], m[2:3]

    def modulate(v):
        return _ln(v) * (1.0 + scale) + shift

    x = x_ref[...]
    buf_ref[0:halo] = jnp.where(pos0 != 0, modulate(xp_ref[...]), 0.0)
    buf_ref[halo:halo + tm] = modulate(x)
    buf_ref[halo + tm:] = jnp.where(pos0 + tm != length, modulate(xn_ref[...]), 0.0)

    t = pos0 + lax.broadcasted_iota(jnp.int32, (tm, 1), 0)
    n = tm + 2 * halo
    for g, half in enumerate(POOL_HALF):
        cols = slice(g * GROUP_W, (g + 1) * GROUP_W)
        hbuf = buf_ref[:, cols]
        wsum = hbuf + pltpu.roll(hbuf, 1, 0)
        h = 1
        while h < half:
            wsum = pltpu.roll(wsum, h, 0) + pltpu.roll(wsum, n - h, 0)
            h *= 2
        cnt = (jnp.minimum(t + half, length) - jnp.maximum(t - half, 0)).astype(F32)
        pooled = wsum[halo:halo + tm] / cnt - hbuf[halo:halo + tm]
        p_ref[:, cols] = _dot(pooled.astype(BF16), w_ref[g])
    y = p_ref[...] * sc_ref[...]
    o_ref[...] = _post_norm(x, y, gate, g_ref[...], b_ref[...])


def _pool(x, mod, ln_g, ln_b, w_grp, scale, layer, j):
    tm, halo = TM_SEQ, POOL_HALO
    nb = N_TOK // halo
    return pl.pallas_call(
        _pool_kernel,
        grid=(N_TOK // tm,),
        in_specs=[
            _row_spec(tm),
            pl.BlockSpec((halo, D_MODEL), lambda i: (jnp.maximum(i * (tm // halo) - 1, 0), 0)),
            pl.BlockSpec((halo, D_MODEL), lambda i: (jnp.minimum((i + 1) * (tm // halo), nb - 1), 0)),
            _mod_spec(tm, layer),
            _layer((N_GROUPS, GROUP_W, GROUP_W), j),
            _layer((1, D_MODEL), j),
            _ln_spec(layer, 0), _ln_spec(layer, 0),
        ],
        out_specs=_row_spec(tm),
        out_shape=jax.ShapeDtypeStruct((N_TOK, D_MODEL), F32),
        scratch_shapes=[
            pltpu.VMEM((tm + 2 * halo, D_MODEL), F32),
            pltpu.VMEM((tm, D_MODEL), F32),
        ],
        compiler_params=_params(),
        name="pool_mixer",
    )(x, x, x, mod, w_grp, scale, ln_g, ln_b)


def _gmlp_kernel(x_ref, mod_ref, win_ref, bin_ref, vg_ref, vb_ref, ws_ref, bs_ref, wo_ref, g_ref, b_ref,
                 o_ref, act_ref, v_ref, *, tm):
    m = mod_ref[...]
    x = x_ref[...]
    h = (_ln(x) * (1.0 + m[1:2]) + m[0:1]).astype(BF16)

    def branch(lo, width):
        return jax.nn.gelu(_dot(h, win_ref[:, lo:lo + width]) + bin_ref[:, lo:lo + width])

    half = GMLP_W // 2
    v = jnp.concatenate([branch(GMLP_W, half), branch(GMLP_W + half, half)], axis=1)
    v_ref[...] = (_ln(v) * vg_ref[...] + vb_ref[...]).astype(BF16)
    bs = bs_ref[...]
    groups_per_half = half // GMLP_GW
    for uc in range(GMLP_W // half):
        u = branch(uc * half, half)
        for gl in range(groups_per_half):
            g = uc * groups_per_half + gl
            cols = slice(g * GMLP_GW, (g + 1) * GMLP_GW)
            for n in range(tm // CHUNK):
                rows = slice(n * CHUNK, (n + 1) * CHUNK)
                s = _dot(ws_ref[g], v_ref[rows, cols]) + bs[:, g:g + 1]
                act_ref[rows, cols] = (u[rows, gl * GMLP_GW:(gl + 1) * GMLP_GW] * s).astype(BF16)
    y = _dot(act_ref[...], wo_ref[...])
    o_ref[...] = _post_norm(x, y, m[2:3], g_ref[...], b_ref[...])


def _gmlp(x, mod, ln_g, ln_b, w_in, b_in, v_g, v_b, w_s, b_s_t, w_o, layer, j):
    tm = TM_GMLP
    return pl.pallas_call(
        functools.partial(_gmlp_kernel, tm=tm),
        grid=(N_TOK // tm,),
        in_specs=[
            _row_spec(tm),
            _mod_spec(tm, layer),
            _layer((D_MODEL, 2 * GMLP_W), j),
            _layer((1, 2 * GMLP_W), j),
            _layer((1, GMLP_W), j), _layer((1, GMLP_W), j),
            _layer((GMLP_GROUPS, CHUNK, CHUNK), j),
            _layer((CHUNK, GMLP_GROUPS), j),
            _layer((GMLP_W, D_MODEL), j),
            _ln_spec(layer, 0), _ln_spec(layer, 0),
        ],
        out_specs=_row_spec(tm),
        out_shape=jax.ShapeDtypeStruct((N_TOK, D_MODEL), F32),
        scratch_shapes=[pltpu.VMEM((tm, GMLP_W), BF16)] * 2,
        compiler_params=_params(),
        name="gmlp_mixer",
    )(x, mod, w_in, b_in, v_g, v_b, w_s, b_s_t, w_o, ln_g, ln_b)


def kernel(x_prompt, x_sample, cache_k, cache_v, c, c_ctx, w_mod, b_mod, ln_g, ln_b, ffn_w_up, ffn_conv_w,
           ffn_conv_b, ffn_w_down, fnet_w_o, attn_w_qkv, attn_q_norm, attn_k_norm, attn_w_o, pool_w,
           pool_scale, gmlp_w_in, gmlp_b_in, gmlp_ln_g, gmlp_ln_b, gmlp_w_s, gmlp_b_s, gmlp_w_o):
    assert DEPTH == 4
    cond = jnp.concatenate([c_ctx[None, :], c, jnp.zeros((COND_PAD - N_COND, D_MODEL), F32)], axis=0)
    mod = _modulation(cond, w_mod, b_mod)
    ln_g = ln_g.reshape(DEPTH * 2, 1, D_MODEL)
    ln_b = ln_b.reshape(DEPTH * 2, 1, D_MODEL)
    conv_b = ffn_conv_b.reshape(DEPTH, 1, D_FF)

    steps_f = N_TOK // 512
    assert 2 * CAST_STEPS <= steps_f
    x, up0, down0, w_qkv, w_ao, w_pool = _fnet(
        x_prompt.reshape(N_P, D_MODEL), x_sample.reshape(N_S, D_MODEL), mod, ln_g, ln_b,
        fnet_w_o.astype(BF16), 0, 0,
        [_cast_plan(ffn_w_up, 0, 1, 0), _cast_plan(ffn_w_down, 0, 1, 0),
         _cast_plan(attn_w_qkv, 0, 1, CAST_STEPS), _cast_plan(attn_w_o, 0, 1, CAST_STEPS),
         _cast_plan(pool_w.reshape(-1, N_GROUPS * GROUP_W, GROUP_W), 0, 1, CAST_STEPS)])
    x = _ffn(x, mod, ln_g, ln_b, up0, ffn_conv_w, conv_b, down0, 0, 0, False)

    rest = DEPTH - 1
    assert (rest + 1) * CAST_STEPS <= N_TOK // TM_SEQ
    x, new_k, new_v, up, down, g_in, g_out = _attention(
        x, mod, ln_g, ln_b, w_qkv, attn_q_norm, attn_k_norm, w_ao, cache_k, cache_v, 1, 0,
        [_cast_plan(ffn_w_up, 1, rest, 0), _cast_plan(ffn_w_down, 1, rest, 0),
         _cast_plan(gmlp_w_in, 0, 1, rest * CAST_STEPS), _cast_plan(gmlp_w_o, 0, 1, rest * CAST_STEPS)])
    x = _ffn(x, mod, ln_g, ln_b, up, ffn_conv_w, conv_b, down, 1, 0, False)
    x = _pool(x, mod, ln_g, ln_b, w_pool.reshape(-1, N_GROUPS, GROUP_W, GROUP_W),
              pool_scale.reshape(-1, 1, D_MODEL), 2, 0)
    x = _ffn(x, mod, ln_g, ln_b, up, ffn_conv_w, conv_b, down, 2, 1, False)
    x = _gmlp(x, mod, ln_g, ln_b, g_in, gmlp_b_in.reshape(-1, 1, 2 * GMLP_W),
              gmlp_ln_g.reshape(-1, 1, GMLP_W), gmlp_ln_b.reshape(-1, 1, GMLP_W), gmlp_w_s.astype(BF16),
              gmlp_b_s.transpose(0, 2, 1), g_out, 3, 0)
    y_prompt, y_sample = _ffn(x, mod, ln_g, ln_b, up, ffn_conv_w, conv_b, down, 3, 2, True)

    return (y_prompt.reshape(BATCH, SEQ, D_MODEL), y_sample.reshape(DEC_BATCH, DEC_SEQ, D_MODEL),
            new_k.reshape(BATCH, 1, SEQ, N_KV_HEADS, HEAD_DIM), new_v.reshape(BATCH, 1, SEQ, N_KV_HEADS, HEAD_DIM))
```

```python
import functools
import math

import numpy as np
import jax
import jax.numpy as jnp
from jax import lax
from jax.experimental import pallas as pl
from jax.experimental.pallas import tpu as pltpu

D_MODEL = 1024
BATCH = 32
SEQ = 256
DEPTH = 4
DEC_BATCH = 2
DEC_SEQ = 4096
PAST_LEN = 512
GRID_W = 64
N_GROUPS = 4
GROUP_W = D_MODEL // N_GROUPS
HEAD_DIM = 128
N_Q_HEADS = D_MODEL // HEAD_DIM
N_KV_HEADS = 2
Q_PER_KV = N_Q_HEADS // N_KV_HEADS
Q_W = N_Q_HEADS * HEAD_DIM
KV_W = N_KV_HEADS * HEAD_DIM
QKV_W = Q_W + 2 * KV_W
ROPE_THETA = 10000.0
POOL_HALF = (1, 2, 4, 8)
CHUNK = 128
GMLP_W = 2 * D_MODEL
GMLP_GROUPS = 4
GMLP_GW = GMLP_W // GMLP_GROUPS
D_FF = ((8 * D_MODEL // 3 + 127) // 128) * 128
ALPHA = (2 * DEPTH) ** 0.25
LN_EPS = 1e-6

N_P = BATCH * SEQ
N_S = DEC_BATCH * DEC_SEQ
N_TOK = N_P + N_S
N_COND = 1 + DEC_BATCH
COND_PAD = 8

F32 = jnp.float32
BF16 = jnp.bfloat16

VMEM_LIMIT_BYTES = 56 * 1024 * 1024
BF16_ROWS = 16

TM_SEQ = SEQ
TM_FFN = 512
TM_QKV = 256
TM_GMLP = 512
LANES = 128
MXU_DIM = 256
FFN_CHUNKS = (5 * MXU_DIM, 6 * MXU_DIM)
assert sum(FFN_CHUNKS) == D_FF
DFT_RADIX = 8
DFT_SUB = DEC_SEQ // DFT_RADIX
DFT_COEF_ROWS = 128
DFT_CH = 512
ATTN_PAIR = 2


def _dot(a, b):
    return jnp.dot(a, b, preferred_element_type=F32)


def _dot_nt(a, b):
    return lax.dot_general(a, b, (((1,), (1,)), ((), ())), preferred_element_type=F32)


def _ln(x):
    mu = jnp.mean(x, axis=-1, keepdims=True)
    xc = x - mu
    var = jnp.mean(xc * xc, axis=-1, keepdims=True)
    return xc * lax.rsqrt(var + LN_EPS)


def _post_norm(x, y, gate, g, b):
    return _ln(ALPHA * x + gate * y) * g + b


def _cond_row(row_start):
    return jnp.maximum(row_start - (N_P - DEC_SEQ), 0) // DEC_SEQ


def _seq_pos(row_start):
    is_p = row_start < N_P
    length = jnp.where(is_p, SEQ, DEC_SEQ)
    pos = jnp.where(is_p, row_start, row_start - N_P) & (length - 1)
    return length, pos


def _params():
    return pltpu.CompilerParams(dimension_semantics=("arbitrary",), vmem_limit_bytes=VMEM_LIMIT_BYTES)


def _resident(shape):
    return pl.BlockSpec(shape, lambda i: (0,) * len(shape), pipeline_mode=pl.Buffered(1))


def _layer(tail, layer):
    return pl.BlockSpec((None,) + tuple(tail), lambda i: (layer,) + (0,) * len(tail),
                        pipeline_mode=pl.Buffered(1))


def _mod_spec(tm, layer):
    return pl.BlockSpec((None, None, 6, D_MODEL), lambda i: (layer, _cond_row(i * tm), 0, 0))


def _ln_spec(layer, which):
    return pl.BlockSpec((None, 1, D_MODEL), lambda i: (2 * layer + which, 0, 0),
                        pipeline_mode=pl.Buffered(1))


def _row_spec(tm, width=D_MODEL):
    return pl.BlockSpec((tm, width), lambda i: (i, 0))


def _stream_specs(tm, width=D_MODEL):
    n_pt = N_P // tm
    return (pl.BlockSpec((tm, width), lambda i: (jnp.minimum(i, n_pt - 1), 0)),
            pl.BlockSpec((tm, width), lambda i: (jnp.maximum(i - n_pt, 0), 0)))


CAST_STEPS = 16


def _cast_plan(w, first_layer, n_layers, first_step):
    _, rows, width = w.shape
    rb = rows // CAST_STEPS
    assert rb * CAST_STEPS == rows and rb % BF16_ROWS == 0

    def block(i):
        j = jnp.clip(i - first_step, 0, n_layers * CAST_STEPS - 1)
        return j // CAST_STEPS, j % CAST_STEPS

    in_spec = pl.BlockSpec((None, rb, width), lambda i: (first_layer + block(i)[0], block(i)[1], 0))
    out_spec = pl.BlockSpec((None, rb, width), lambda i: (block(i)[0], block(i)[1], 0))
    return w, in_spec, out_spec, jax.ShapeDtypeStruct((n_layers, rows, width), BF16)


def _run_casts(src_refs, dst_refs):
    for src, dst in zip(src_refs, dst_refs, strict=True):
        dst[...] = src[...].astype(BF16)


def _mod_kernel(c_ref, w_ref, b_ref, o_ref):
    c = c_ref[...]
    s = c / (1.0 + jnp.exp(-c))
    o_ref[...] = _dot(s.astype(BF16), w_ref[...].astype(BF16)) + b_ref[...]


def _modulation(cond, w_mod, b_mod):
    tn = 2048
    out = pl.pallas_call(
        _mod_kernel,
        grid=(DEPTH, 6 * D_MODEL // tn),
        in_specs=[
            pl.BlockSpec((COND_PAD, D_MODEL), lambda l, j: (0, 0)),
            pl.BlockSpec((None, D_MODEL, tn), lambda l, j: (l, 0, j)),
            pl.BlockSpec((None, 1, tn), lambda l, j: (l, 0, j)),
        ],
        out_specs=pl.BlockSpec((None, COND_PAD, tn), lambda l, j: (l, 0, j)),
        out_shape=jax.ShapeDtypeStruct((DEPTH, COND_PAD, 6 * D_MODEL), F32),
        compiler_params=pltpu.CompilerParams(
            dimension_semantics=("arbitrary", "arbitrary"), vmem_limit_bytes=VMEM_LIMIT_BYTES),
        name="modulation",
    )(cond, w_mod, b_mod.reshape(DEPTH, 1, 6 * D_MODEL))
    return out[:, :N_COND].reshape(DEPTH, N_COND, 6, D_MODEL)


def _ffn_kernel(x_ref, xp_ref, xn_ref, mod_ref, g_ref, b_ref, wup_ref, cw_ref, cb_ref, wdn_ref,
                *rest, tm, split_out):
    halo = BF16_ROWS
    out_refs, (hext_ref, aext_ref) = rest[:-2], rest[-2:]
    start = pl.program_id(0) * tm
    length, pos0 = _seq_pos(start)
    m = mod_ref[...]
    shift, scale, gate = m[3:4], m[4:5], m[5:6]

    def modulate(v):
        return (_ln(v) * (1.0 + scale) + shift).astype(BF16)

    x = x_ref[...]
    hext_ref[0:halo] = modulate(xp_ref[...])
    hext_ref[halo:halo + tm] = modulate(x)
    hext_ref[halo + tm:] = modulate(xn_ref[...])

    t = (pos0 + lax.broadcasted_iota(jnp.int32, (tm, 1), 0)) & (length - 1)
    has_prev = t != 0
    has_next = t != length - 1

    acc = jnp.zeros((tm, D_MODEL), F32)
    lo = 0
    for fc in FFN_CHUNKS:
        aext_ref[:, 0:fc] = _dot(hext_ref[...], wup_ref[:, lo:lo + fc])
        bgate = _dot(hext_ref[halo:halo + tm], wup_ref[:, D_FF + lo:D_FF + lo + fc])
        cw = cw_ref[:, lo:lo + fc]
        a = (jnp.where(has_prev, aext_ref[halo - 1:halo - 1 + tm, 0:fc], 0.0) * cw[0:1]
             + aext_ref[halo:halo + tm, 0:fc] * cw[1:2]
             + jnp.where(has_next, aext_ref[halo + 1:halo + 1 + tm, 0:fc], 0.0) * cw[2:3]
             + cb_ref[:, lo:lo + fc])
        act = (jax.nn.gelu(a) * bgate).astype(BF16)
        acc = acc + _dot(act, wdn_ref[lo:lo + fc, :])
        lo += fc
    y = _post_norm(x, acc, gate, g_ref[...], b_ref[...])

    if split_out:
        @pl.when(start < N_P)
        def _():
            out_refs[0][...] = y

        @pl.when(start >= N_P)
        def _():
            out_refs[1][...] = y
    else:
        out_refs[0][...] = y


def _ffn(x, mod, ln_g, ln_b, w_up, conv_w, conv_b, w_down, layer, w_layer, split_out):
    tm, halo = TM_FFN, BF16_ROWS
    nb = N_TOK // halo
    if split_out:
        out_specs = list(_stream_specs(tm))
        out_shape = [jax.ShapeDtypeStruct((N_P, D_MODEL), F32), jax.ShapeDtypeStruct((N_S, D_MODEL), F32)]
    else:
        out_specs = _row_spec(tm)
        out_shape = jax.ShapeDtypeStruct((N_TOK, D_MODEL), F32)
    return pl.pallas_call(
        functools.partial(_ffn_kernel, tm=tm, split_out=split_out),
        grid=(N_TOK // tm,),
        in_specs=[
            _row_spec(tm),
            pl.BlockSpec((halo, D_MODEL), lambda i: (jnp.maximum(i * (tm // halo) - 1, 0), 0)),
            pl.BlockSpec((halo, D_MODEL), lambda i: (jnp.minimum((i + 1) * (tm // halo), nb - 1), 0)),
            _mod_spec(tm, layer),
            _ln_spec(layer, 1), _ln_spec(layer, 1),
            _layer((D_MODEL, 2 * D_FF), w_layer),
            _layer((3, D_FF), layer),
            _layer((1, D_FF), layer),
            _layer((D_FF, D_MODEL), w_layer),
        ],
        out_specs=out_specs,
        out_shape=out_shape,
        scratch_shapes=[
            pltpu.VMEM((tm + 2 * halo, D_MODEL), BF16),
            pltpu.VMEM((tm + 2 * halo, max(FFN_CHUNKS)), F32),
        ],
        compiler_params=_params(),
        name="conv_ffn",
    )(x, x, x, mod, ln_g, ln_b, w_up, conv_w, conv_b, w_down)


def _dft_matrix(n):
    k = np.arange(n, dtype=np.int64)
    ang = 2.0 * np.pi * ((k[:, None] * k[None, :]) % n) / n
    return np.cos(ang).astype(np.float32), np.sin(ang).astype(np.float32)


def _dft_twiddle_tables():
    t1 = np.arange(DFT_RADIX, dtype=np.int64)[:, None]
    k2 = np.arange(DFT_SUB, dtype=np.int64)[None, :]
    ang = 2.0 * np.pi * (t1 * k2) / DEC_SEQ
    shape = (DFT_RADIX, DFT_SUB // DFT_COEF_ROWS, DFT_COEF_ROWS)
    pad = ((0, 0), (0, DFT_COEF_ROWS - shape[1]), (0, 0))
    f = lambda a: np.pad(a.reshape(shape), pad).astype(np.float32)
    return f(np.cos(ang)), f(np.sin(ang))


def _octant(n):
    r = math.sqrt(0.5)
    return [(1.0, 0.0), (r, r), (0.0, 1.0), (-r, r), (-1.0, 0.0), (-r, -r), (0.0, -1.0), (r, -r)][n % 8]


def _fnet_a_kernel(xp_ref, xs_ref, mod_ref, cs_ref, pp_ref, qp_ref, pd_ref, qd_ref, p_scr, q_scr, *, tm):
    is_prompt = pl.program_id(0) < N_P // tm
    x = jnp.where(is_prompt, xp_ref[...], xs_ref[...])
    m = mod_ref[...]
    h = (_ln(x) * (1.0 + m[1:2]) + m[0:1]).astype(BF16)
    per_group = GROUP_W // LANES
    for g in range(N_GROUPS):
        r = _dot(h[:, g * GROUP_W:(g + 1) * GROUP_W], cs_ref[...])
        for c in range(per_group):
            p_scr[g * per_group + c] = r[:, c * LANES:(c + 1) * LANES]
            q_scr[g * per_group + c] = r[:, GROUP_W + c * LANES:GROUP_W + (c + 1) * LANES]

    @pl.when(is_prompt)
    def _():
        for c in range(D_MODEL // LANES):
            pp_ref[:, c * LANES:(c + 1) * LANES] = p_scr[c].astype(BF16)
            qp_ref[:, c * LANES:(c + 1) * LANES] = q_scr[c].astype(BF16)

    @pl.when(jnp.logical_not(is_prompt))
    def _():
        for t1 in range(DFT_RADIX):
            rows = pl.ds(t1, tm // DFT_RADIX, stride=DFT_RADIX)
            for c in range(D_MODEL // LANES):
                pd_ref[t1, :, c * LANES:(c + 1) * LANES] = p_scr[c, rows, :].astype(BF16)
                qd_ref[t1, :, c * LANES:(c + 1) * LANES] = q_scr[c, rows, :].astype(BF16)


def _fnet_latent_kernel(pd_ref, qd_ref, cs_ref, ct_ref, st_ref, f_ref, a_ref, b_ref):
    for t1 in range(DFT_RADIX):
        csp = _dot(cs_ref[...], pd_ref[t1])
        csq = _dot(cs_ref[...], qd_ref[t1])
        u = csp[:DFT_SUB] - csq[DFT_SUB:]
        v = csq[:DFT_SUB] + csp[DFT_SUB:]
        if t1 == 0:
            a_ref[0], b_ref[0] = u, v
            continue
        ct, st = ct_ref[t1].T, st_ref[t1].T
        for blk in range(DFT_SUB // DFT_COEF_ROWS):
            rows = slice(blk * DFT_COEF_ROWS, (blk + 1) * DFT_COEF_ROWS)
            cc, ss = ct[:, blk:blk + 1], st[:, blk:blk + 1]
            a_ref[t1, rows, :] = cc * u[rows] - ss * v[rows]
            b_ref[t1, rows, :] = ss * u[rows] + cc * v[rows]

    norm = (DEC_SEQ * GROUP_W) ** -0.5
    for k1 in range(DFT_RADIX):
        for blk in range(DFT_SUB // DFT_COEF_ROWS):
            rows = slice(blk * DFT_COEF_ROWS, (blk + 1) * DFT_COEF_ROWS)
            acc = a_ref[0, rows, :]
            for t1 in range(1, DFT_RADIX):
                c, s = _octant(k1 * t1)
                if s == 0.0:
                    acc = acc + a_ref[t1, rows, :] if c > 0 else acc - a_ref[t1, rows, :]
                elif c == 0.0:
                    acc = acc - b_ref[t1, rows, :] if s > 0 else acc + b_ref[t1, rows, :]
                else:
                    acc = acc + (c * a_ref[t1, rows, :] - s * b_ref[t1, rows, :])
            out_rows = slice(k1 * DFT_SUB + blk * DFT_COEF_ROWS, k1 * DFT_SUB + (blk + 1) * DFT_COEF_ROWS)
            f_ref[out_rows, :] = (acc * norm).astype(BF16)


def _fnet_b_kernel(xp_ref, xs_ref, pp_ref, qp_ref, fs_ref, c256_ref, s256_ref, wo_ref, mod_ref, g_ref,
                   b_ref, *rest, tm, n_cast):
    cast_src, o_ref, cast_dst, f_ref = rest[:n_cast], rest[n_cast], rest[n_cast + 1:-1], rest[-1]
    _run_casts(cast_src, cast_dst)
    is_prompt = pl.program_id(0) < N_P // tm

    @pl.when(is_prompt)
    def _():
        for s in range(tm // SEQ):
            rows = slice(s * SEQ, (s + 1) * SEQ)
            f = _dot(c256_ref[...], pp_ref[rows, :]) - _dot(s256_ref[...], qp_ref[rows, :])
            f_ref[rows, :] = (f * (SEQ * GROUP_W) ** -0.5).astype(BF16)

    @pl.when(jnp.logical_not(is_prompt))
    def _():
        f_ref[...] = fs_ref[...]

    x = jnp.where(is_prompt, xp_ref[...], xs_ref[...])
    y = _dot(f_ref[...], wo_ref[...])
    o_ref[...] = _post_norm(x, y, mod_ref[...][2:3], g_ref[...], b_ref[...])


def _fnet(xp, xs, mod, ln_g, ln_b, w_o, layer, j, casts):
    c256, s256 = _dft_matrix(SEQ)
    cs = jnp.concatenate([jnp.asarray(c256), jnp.asarray(s256)], axis=1).astype(BF16)
    tm = 512
    n_pt = N_P // tm
    per_seq = DEC_SEQ // tm
    prompt_rows = _stream_specs(tm)[0]
    slabs = pl.BlockSpec((None, DFT_RADIX, tm // DFT_RADIX, D_MODEL),
                         lambda i: (jnp.maximum(i - n_pt, 0) // per_seq, 0, jnp.maximum(i - n_pt, 0) % per_seq, 0))
    dec_shape = jax.ShapeDtypeStruct((DEC_BATCH, DFT_RADIX, DFT_SUB, D_MODEL), BF16)
    pp, qp, pd, qd = pl.pallas_call(
        functools.partial(_fnet_a_kernel, tm=tm),
        grid=(N_TOK // tm,),
        in_specs=[*_stream_specs(tm), _mod_spec(tm, layer), _resident((GROUP_W, 2 * GROUP_W))],
        out_specs=[prompt_rows, prompt_rows, slabs, slabs],
        out_shape=[jax.ShapeDtypeStruct((N_P, D_MODEL), BF16)] * 2 + [dec_shape] * 2,
        scratch_shapes=[pltpu.VMEM((D_MODEL // LANES, tm, LANES), F32)] * 2,
        compiler_params=_params(),
        name="fnet_channel_dft",
    )(xp, xs, mod, cs)

    c_sub, s_sub = _dft_matrix(DFT_SUB)
    cs_sub = jnp.concatenate([jnp.asarray(c_sub), jnp.asarray(s_sub)], axis=0).astype(BF16)
    ct, st = _dft_twiddle_tables()
    slabs_in = pl.BlockSpec((None, DFT_RADIX, DFT_SUB, DFT_CH), lambda b, c: (b, 0, 0, c))
    whole = lambda shape: pl.BlockSpec(shape, lambda b, c: (0,) * len(shape), pipeline_mode=pl.Buffered(1))
    f_s = pl.pallas_call(
        _fnet_latent_kernel,
        grid=(DEC_BATCH, D_MODEL // DFT_CH),
        in_specs=[slabs_in, slabs_in, whole((2 * DFT_SUB, DFT_SUB)), whole(ct.shape), whole(st.shape)],
        out_specs=pl.BlockSpec((DEC_SEQ, DFT_CH), lambda b, c: (b, c)),
        out_shape=jax.ShapeDtypeStruct((N_S, D_MODEL), BF16),
        scratch_shapes=[pltpu.VMEM((DFT_RADIX, DFT_SUB, DFT_CH), F32)] * 2,
        compiler_params=pltpu.CompilerParams(
            dimension_semantics=("arbitrary",) * 2, vmem_limit_bytes=VMEM_LIMIT_BYTES),
        name="fnet_latent_dft",
    )(pd, qd, cs_sub, jnp.asarray(ct), jnp.asarray(st))

    prompt_rows, latent_rows = _stream_specs(tm)
    return pl.pallas_call(
        functools.partial(_fnet_b_kernel, tm=tm, n_cast=len(casts)),
        grid=(N_TOK // tm,),
        in_specs=[
            *_stream_specs(tm),
            prompt_rows, prompt_rows, latent_rows,
            _resident((SEQ, SEQ)), _resident((SEQ, SEQ)),
            _layer((D_MODEL, D_MODEL), j),
            _mod_spec(tm, layer),
            _ln_spec(layer, 0), _ln_spec(layer, 0),
            *[c[1] for c in casts],
        ],
        out_specs=[_row_spec(tm), *[c[2] for c in casts]],
        out_shape=[jax.ShapeDtypeStruct((N_TOK, D_MODEL), F32), *[c[3] for c in casts]],
        scratch_shapes=[pltpu.VMEM((tm, D_MODEL), BF16)],
        compiler_params=_params(),
        name="fnet_token_dft",
    )(xp, xs, pp, qp, f_s, jnp.asarray(c256).astype(BF16), jnp.asarray(s256).astype(BF16), w_o, mod,
      ln_g, ln_b, *[c[0] for c in casts])


def _head_perm():
    quarter = HEAD_DIM // 4
    d = np.arange(HEAD_DIM).reshape(4, quarter)
    return d[[0, 2, 1, 3]].reshape(HEAD_DIM)


def _to_head_perm(a):
    q = HEAD_DIM // 4
    return jnp.concatenate([a[..., :q], a[..., 2 * q:3 * q], a[..., q:2 * q], a[..., 3 * q:]], axis=-1)


def _rope_tables(tm):
    quarter = HEAD_DIM // 4
    t = np.arange(DEC_SEQ)
    row, col = (t // GRID_W).astype(np.float32), (t % GRID_W).astype(np.float32)
    inv = (np.float32(ROPE_THETA) ** (-np.arange(quarter, dtype=np.float32) / np.float32(quarter))).astype(np.float32)
    ang_r = (row[:, None] * inv[None, :]).astype(np.float32).astype(np.float64)
    ang_c = (col[:, None] * inv[None, :]).astype(np.float32).astype(np.float64)
    cos = np.concatenate([np.cos(ang_r), np.cos(ang_r), np.cos(ang_c), np.cos(ang_c)], axis=1)
    sin = np.concatenate([-np.sin(ang_r), np.sin(ang_r), -np.sin(ang_c), np.sin(ang_c)], axis=1)
    cos = np.concatenate([cos, np.ones((tm, HEAD_DIM))], axis=0)[:, _head_perm()]
    sin = np.concatenate([sin, np.zeros((tm, HEAD_DIM))], axis=0)[:, _head_perm()]
    return np.asarray(cos, np.float32), np.asarray(sin, np.float32)


def _qkv_kernel(x_ref, mod_ref, w_ref, qg_ref, kg_ref, cos_ref, sin_ref,
                q_ref, k_ref, v_ref, vt_ref, nk_ref, nv_ref, *, tm):
    i = pl.program_id(0)
    m = mod_ref[...]
    h = (_ln(x_ref[...]) * (1.0 + m[1:2]) + m[0:1]).astype(BF16)
    cos, sin = cos_ref[...], sin_ref[...]

    def rms(v, g):
        return v * lax.rsqrt(jnp.mean(v * v, axis=-1, keepdims=True) + LN_EPS) * g

    def rope(v):
        return v * cos + pltpu.roll(v, HEAD_DIM // 2, 1) * sin

    def project(lo):
        return _dot(h, w_ref[:, lo:lo + MXU_DIM])

    heads_per_block = MXU_DIM // HEAD_DIM
    for blk in range(Q_W // MXU_DIM):
        qq = project(blk * MXU_DIM)
        for j in range(heads_per_block):
            sl = slice((blk * heads_per_block + j) * HEAD_DIM, (blk * heads_per_block + j + 1) * HEAD_DIM)
            q_ref[:, sl] = rope(rms(qq[:, j * HEAD_DIM:(j + 1) * HEAD_DIM], qg_ref[...])).astype(BF16)
    kk = project(Q_W)
    kn = [rms(kk[:, hk * HEAD_DIM:(hk + 1) * HEAD_DIM], kg_ref[...]) for hk in range(N_KV_HEADS)]
    for hk in range(N_KV_HEADS):
        k_ref[:, hk * HEAD_DIM:(hk + 1) * HEAD_DIM] = rope(kn[hk]).astype(BF16)
    v = project(Q_W + KV_W)
    v_ref[...] = v.astype(BF16)

    is_prompt = i < N_P // tm

    @pl.when(is_prompt)
    def _():
        for hk in range(N_KV_HEADS):
            nk_ref[:, hk * HEAD_DIM:(hk + 1) * HEAD_DIM] = kn[hk]
        nv_ref[...] = v

    @pl.when(jnp.logical_not(is_prompt))
    def _():
        vt_ref[...] = v.T.astype(BF16)


def _attn_kernel(x_ref, q_ref, ks_ref, vs_ref, kb_ref, vbt_ref, kc_ref, vct_ref, wo_ref, mod_ref, g_ref,
                 b_ref, *rest, n_cast):
    cast_src, o_ref, cast_dst, att_ref = rest[:n_cast], rest[n_cast], rest[n_cast + 1:-1], rest[-1]
    _run_casts(cast_src, cast_dst)
    i = pl.program_id(0)
    tm = TM_SEQ
    is_prompt = i < N_P // tm

    def head(hq):
        return slice(hq * HEAD_DIM, (hq + 1) * HEAD_DIM)

    @pl.when(is_prompt)
    def _():
        for hk in range(N_KV_HEADS):
            q4 = jnp.concatenate([q_ref[:, head(hk * Q_PER_KV + g)] for g in range(Q_PER_KV)], axis=0)
            s = _dot_nt(q4, ks_ref[:, head(hk)])
            p = jnp.exp2(s - jnp.max(s, axis=-1, keepdims=True))
            o4 = _dot(p.astype(BF16), vs_ref[:, head(hk)]) / jnp.sum(p, axis=-1, keepdims=True)
            for g in range(Q_PER_KV):
                att_ref[:, head(hk * Q_PER_KV + g)] = o4[g * tm:(g + 1) * tm].astype(BF16)

    @pl.when(jnp.logical_not(is_prompt))
    def _():
        for hk in range(N_KV_HEADS):
            for g0 in range(0, Q_PER_KV, ATTN_PAIR):
                hq0 = hk * Q_PER_KV + g0
                qq = jnp.concatenate([q_ref[:, head(hq0 + g)] for g in range(ATTN_PAIR)], axis=0)
                st_b = _dot_nt(kb_ref[:, head(hk)], qq)
                st_c = _dot_nt(kc_ref[:, head(hk)], qq)
                mx = jnp.maximum(jnp.max(st_b, axis=0, keepdims=True), jnp.max(st_c, axis=0, keepdims=True))
                pt_b = jnp.exp2(st_b - mx)
                pt_c = jnp.exp2(st_c - mx)
                den = jnp.sum(pt_b, axis=0, keepdims=True) + jnp.sum(pt_c, axis=0, keepdims=True)
                ot = (_dot(vbt_ref[head(hk), :], pt_b.astype(BF16))
                      + _dot(vct_ref[head(hk), :], pt_c.astype(BF16))) / den
                oo = ot.T
                for g in range(ATTN_PAIR):
                    att_ref[:, head(hq0 + g)] = oo[g * tm:(g + 1) * tm].astype(BF16)

    y = _dot(att_ref[...], wo_ref[...])
    o_ref[...] = _post_norm(x_ref[...], y, mod_ref[...][2:3], g_ref[...], b_ref[...])


def _attention(x, mod, ln_g, ln_b, w_qkv, q_norm, k_norm, w_o, cache_k, cache_v, layer, j, casts):
    tm = TM_QKV
    n_pt = N_P // tm
    cos, sin = _rope_tables(tm)
    rope_spec = pl.BlockSpec(
        (tm, HEAD_DIM),
        lambda i: (jnp.where(i < n_pt, DEC_SEQ // tm, jnp.maximum(i - n_pt, 0) % (DEC_SEQ // tm)), 0))
    prompt_row = _stream_specs(tm, KV_W)[0]
    latent_col = pl.BlockSpec((KV_W, tm), lambda i: (0, jnp.maximum(i - n_pt, 0)))
    q_gain = _to_head_perm(q_norm[j] * (math.log2(math.e) * HEAD_DIM ** -0.5))
    k_gain = _to_head_perm(k_norm[j])
    q, k, v, vt, new_k, new_v = pl.pallas_call(
        functools.partial(_qkv_kernel, tm=tm),
        grid=(N_TOK // tm,),
        in_specs=[
            _row_spec(tm),
            _mod_spec(tm, layer),
            _layer((D_MODEL, QKV_W), j),
            _resident((1, HEAD_DIM)), _resident((1, HEAD_DIM)),
            rope_spec, rope_spec,
        ],
        out_specs=[_row_spec(tm, Q_W), _row_spec(tm, KV_W), _row_spec(tm, KV_W), latent_col,
                   prompt_row, prompt_row],
        out_shape=[
            jax.ShapeDtypeStruct((N_TOK, Q_W), BF16),
            jax.ShapeDtypeStruct((N_TOK, KV_W), BF16),
            jax.ShapeDtypeStruct((N_TOK, KV_W), BF16),
            jax.ShapeDtypeStruct((KV_W, N_S), BF16),
            jax.ShapeDtypeStruct((N_P, KV_W), F32),
            jax.ShapeDtypeStruct((N_P, KV_W), F32),
        ],
        compiler_params=_params(),
        name="qkv_rope",
    )(x, mod, w_qkv, q_gain.reshape(1, HEAD_DIM), k_gain.reshape(1, HEAD_DIM), jnp.asarray(cos),
      jnp.asarray(sin))
    new_k = _to_head_perm(new_k.reshape(N_P, N_KV_HEADS, HEAD_DIM))

    tm = TM_SEQ
    n_pt = N_P // tm
    per_seq = DEC_SEQ // tm
    batch_of = lambda i: jnp.maximum(i - n_pt, 0) // per_seq
    small = _stream_specs(tm, KV_W)[0]
    k_big = pl.BlockSpec((DEC_SEQ, KV_W), lambda i: (N_P // DEC_SEQ + batch_of(i), 0))
    vt_big = pl.BlockSpec((KV_W, DEC_SEQ), lambda i: (0, batch_of(i)))
    kc_spec = pl.BlockSpec((None, PAST_LEN, KV_W), lambda i: (batch_of(i), 0, 0))
    vct_spec = pl.BlockSpec((None, KV_W, PAST_LEN), lambda i: (batch_of(i), 0, 0))
    kc = _to_head_perm(cache_k[:, j]).reshape(DEC_BATCH, PAST_LEN, KV_W).astype(BF16)
    vct = cache_v[:, j].reshape(DEC_BATCH, PAST_LEN, KV_W).transpose(0, 2, 1).astype(BF16)
    x_new, *cast_out = pl.pallas_call(
        functools.partial(_attn_kernel, n_cast=len(casts)),
        grid=(N_TOK // tm,),
        in_specs=[
            _row_spec(tm), _row_spec(tm, Q_W),
            small, small, k_big, vt_big, kc_spec, vct_spec,
            _layer((Q_W, D_MODEL), j),
            _mod_spec(tm, layer),
            _ln_spec(layer, 0), _ln_spec(layer, 0),
            *[c[1] for c in casts],
        ],
        out_specs=[_row_spec(tm), *[c[2] for c in casts]],
        out_shape=[jax.ShapeDtypeStruct((N_TOK, D_MODEL), F32), *[c[3] for c in casts]],
        scratch_shapes=[pltpu.VMEM((tm, Q_W), BF16)],
        compiler_params=_params(),
        name="attention",
    )(x, q, k, v, k, vt, kc, vct, w_o, mod, ln_g, ln_b, *[c[0] for c in casts])
    return (x_new, new_k, new_v, *cast_out)


POOL_HALO = 8


def _pool_kernel(x_ref, xp_ref, xn_ref, mod_ref, w_ref, sc_ref, g_ref, b_ref, o_ref, buf_ref, p_ref):
    tm, halo = TM_SEQ, POOL_HALO
    start = pl.program_id(0) * tm
    length, pos0 = _seq_pos(start)
    m = mod_ref[...]
    shift, scale, gate = m[0:1], m[1:2], m[2:3]

    def modulate(v):
        return _ln(v) * (1.0 + scale) + shift

    x = x_ref[...]
    buf_ref[0:halo] = jnp.where(pos0 != 0, modulate(xp_ref[...]), 0.0)
    buf_ref[halo:halo + tm] = modulate(x)
    buf_ref[halo + tm:] = jnp.where(pos0 + tm != length, modulate(xn_ref[...]), 0.0)

    t = pos0 + lax.broadcasted_iota(jnp.int32, (tm, 1), 0)
    n = tm + 2 * halo
    for g, half in enumerate(POOL_HALF):
        cols = slice(g * GROUP_W, (g + 1) * GROUP_W)
        hbuf = buf_ref[:, cols]
        wsum = hbuf + pltpu.roll(hbuf, 1, 0)
        h = 1
        while h < half:
            wsum = pltpu.roll(wsum, h, 0) + pltpu.roll(wsum, n - h, 0)
            h *= 2
        cnt = (jnp.minimum(t + half, length) - jnp.maximum(t - half, 0)).astype(F32)
        pooled = wsum[halo:halo + tm] / cnt - hbuf[halo:halo + tm]
        p_ref[:, cols] = _dot(pooled.astype(BF16), w_ref[g])
    y = p_ref[...] * sc_ref[...]
    o_ref[...] = _post_norm(x, y, gate, g_ref[...], b_ref[...])


def _pool(x, mod, ln_g, ln_b, w_grp, scale, layer, j):
    tm, halo = TM_SEQ, POOL_HALO
    nb = N_TOK // halo
    return pl.pallas_call(
        _pool_kernel,
        grid=(N_TOK // tm,),
        in_specs=[
            _row_spec(tm),
            pl.BlockSpec((halo, D_MODEL), lambda i: (jnp.maximum(i * (tm // halo) - 1, 0), 0)),
            pl.BlockSpec((halo, D_MODEL), lambda i: (jnp.minimum((i + 1) * (tm // halo), nb - 1), 0)),
            _mod_spec(tm, layer),
            _layer((N_GROUPS, GROUP_W, GROUP_W), j),
            _layer((1, D_MODEL), j),
            _ln_spec(layer, 0), _ln_spec(layer, 0),
        ],
        out_specs=_row_spec(tm),
        out_shape=jax.ShapeDtypeStruct((N_TOK, D_MODEL), F32),
        scratch_shapes=[
            pltpu.VMEM((tm + 2 * halo, D_MODEL), F32),
            pltpu.VMEM((tm, D_MODEL), F32),
        ],
        compiler_params=_params(),
        name="pool_mixer",
    )(x, x, x, mod, w_grp, scale, ln_g, ln_b)


def _gmlp_kernel(x_ref, mod_ref, win_ref, bin_ref, vg_ref, vb_ref, ws_ref, bs_ref, wo_ref, g_ref, b_ref,
                 o_ref, act_ref, v_ref, *, tm):
    m = mod_ref[...]
    x = x_ref[...]
    h = (_ln(x) * (1.0 + m[1:2]) + m[0:1]).astype(BF16)

    def branch(lo, width):
        return jax.nn.gelu(_dot(h, win_ref[:, lo:lo + width]) + bin_ref[:, lo:lo + width])

    half = GMLP_W // 2
    v = jnp.concatenate([branch(GMLP_W, half), branch(GMLP_W + half, half)], axis=1)
    v_ref[...] = (_ln(v) * vg_ref[...] + vb_ref[...]).astype(BF16)
    bs = bs_ref[...]
    groups_per_half = half // GMLP_GW
    for uc in range(GMLP_W // half):
        u = branch(uc * half, half)
        for gl in range(groups_per_half):
            g = uc * groups_per_half + gl
            cols = slice(g * GMLP_GW, (g + 1) * GMLP_GW)
            for n in range(tm // CHUNK):
                rows = slice(n * CHUNK, (n + 1) * CHUNK)
                s = _dot(ws_ref[g], v_ref[rows, cols]) + bs[:, g:g + 1]
                act_ref[rows, cols] = (u[rows, gl * GMLP_GW:(gl + 1) * GMLP_GW] * s).astype(BF16)
    y = _dot(act_ref[...], wo_ref[...])
    o_ref[...] = _post_norm(x, y, m[2:3], g_ref[...], b_ref[...])


def _gmlp(x, mod, ln_g, ln_b, w_in, b_in, v_g, v_b, w_s, b_s_t, w_o, layer, j):
    tm = TM_GMLP
    return pl.pallas_call(
        functools.partial(_gmlp_kernel, tm=tm),
        grid=(N_TOK // tm,),
        in_specs=[
            _row_spec(tm),
            _mod_spec(tm, layer),
            _layer((D_MODEL, 2 * GMLP_W), j),
            _layer((1, 2 * GMLP_W), j),
            _layer((1, GMLP_W), j), _layer((1, GMLP_W), j),
            _layer((GMLP_GROUPS, CHUNK, CHUNK), j),
            _layer((CHUNK, GMLP_GROUPS), j),
            _layer((GMLP_W, D_MODEL), j),
            _ln_spec(layer, 0), _ln_spec(layer, 0),
        ],
        out_specs=_row_spec(tm),
        out_shape=jax.ShapeDtypeStruct((N_TOK, D_MODEL), F32),
        scratch_shapes=[pltpu.VMEM((tm, GMLP_W), BF16)] * 2,
        compiler_params=_params(),
        name="gmlp_mixer",
    )(x, mod, w_in, b_in, v_g, v_b, w_s, b_s_t, w_o, ln_g, ln_b)


def kernel(x_prompt, x_sample, cache_k, cache_v, c, c_ctx, w_mod, b_mod, ln_g, ln_b, ffn_w_up, ffn_conv_w,
           ffn_conv_b, ffn_w_down, fnet_w_o, attn_w_qkv, attn_q_norm, attn_k_norm, attn_w_o, pool_w,
           pool_scale, gmlp_w_in, gmlp_b_in, gmlp_ln_g, gmlp_ln_b, gmlp_w_s, gmlp_b_s, gmlp_w_o):
    assert DEPTH == 4
    cond = jnp.concatenate([c_ctx[None, :], c, jnp.zeros((COND_PAD - N_COND, D_MODEL), F32)], axis=0)
    mod = _modulation(cond, w_mod, b_mod)
    ln_g = ln_g.reshape(DEPTH * 2, 1, D_MODEL)
    ln_b = ln_b.reshape(DEPTH * 2, 1, D_MODEL)
    conv_b = ffn_conv_b.reshape(DEPTH, 1, D_FF)

    steps_f = N_TOK // 512
    assert 2 * CAST_STEPS <= steps_f
    w_qk = attn_w_qkv[..., :Q_W + KV_W].reshape(-1, D_MODEL, N_Q_HEADS + N_KV_HEADS, HEAD_DIM)
    attn_w_qkv = jnp.concatenate(
        [_to_head_perm(w_qk).reshape(-1, D_MODEL, Q_W + KV_W), attn_w_qkv[..., Q_W + KV_W:]], axis=-1)
    x, up0, down0, w_qkv, w_ao, w_pool = _fnet(
        x_prompt.reshape(N_P, D_MODEL), x_sample.reshape(N_S, D_MODEL), mod, ln_g, ln_b,
        fnet_w_o.astype(BF16), 0, 0,
        [_cast_plan(ffn_w_up, 0, 1, 0), _cast_plan(ffn_w_down, 0, 1, 0),
         _cast_plan(attn_w_qkv, 0, 1, CAST_STEPS), _cast_plan(attn_w_o, 0, 1, CAST_STEPS),
         _cast_plan(pool_w.reshape(-1, N_GROUPS * GROUP_W, GROUP_W), 0, 1, CAST_STEPS)])
    x = _ffn(x, mod, ln_g, ln_b, up0, ffn_conv_w, conv_b, down0, 0, 0, False)

    rest = DEPTH - 1
    assert (rest + 1) * CAST_STEPS <= N_TOK // TM_SEQ
    x, new_k, new_v, up, down, g_in, g_out = _attention(
        x, mod, ln_g, ln_b, w_qkv, attn_q_norm, attn_k_norm, w_ao, cache_k, cache_v, 1, 0,
        [_cast_plan(ffn_w_up, 1, rest, 0), _cast_plan(ffn_w_down, 1, rest, 0),
         _cast_plan(gmlp_w_in, 0, 1, rest * CAST_STEPS), _cast_plan(gmlp_w_o, 0, 1, rest * CAST_STEPS)])
    x = _ffn(x, mod, ln_g, ln_b, up, ffn_conv_w, conv_b, down, 1, 0, False)
    x = _pool(x, mod, ln_g, ln_b, w_pool.reshape(-1, N_GROUPS, GROUP_W, GROUP_W),
              pool_scale.reshape(-1, 1, D_MODEL), 2, 0)
    x = _ffn(x, mod, ln_g, ln_b, up, ffn_conv_w, conv_b, down, 2, 1, False)
    x = _gmlp(x, mod, ln_g, ln_b, g_in, gmlp_b_in.reshape(-1, 1, 2 * GMLP_W),
              gmlp_ln_g.reshape(-1, 1, GMLP_W), gmlp_ln_b.reshape(-1, 1, GMLP_W), gmlp_w_s.astype(BF16),
              gmlp_b_s.transpose(0, 2, 1), g_out, 3, 0)
    y_prompt, y_sample = _ffn(x, mod, ln_g, ln_b, up, ffn_conv_w, conv_b, down, 3, 2, True)

    return (y_prompt.reshape(BATCH, SEQ, D_MODEL), y_sample.reshape(DEC_BATCH, DEC_SEQ, D_MODEL),
            new_k.reshape(BATCH, 1, SEQ, N_KV_HEADS, HEAD_DIM), new_v.reshape(BATCH, 1, SEQ, N_KV_HEADS, HEAD_DIM))
```

```python
import functools
import math

import numpy as np
import jax
import jax.numpy as jnp
from jax import lax
from jax.experimental import pallas as pl
from jax.experimental.pallas import tpu as pltpu

D_MODEL = 1024
BATCH = 32
SEQ = 256
DEPTH = 4
DEC_BATCH = 2
DEC_SEQ = 4096
PAST_LEN = 512
GRID_W = 64
N_GROUPS = 4
GROUP_W = D_MODEL // N_GROUPS
HEAD_DIM = 128
N_Q_HEADS = D_MODEL // HEAD_DIM
N_KV_HEADS = 2
Q_PER_KV = N_Q_HEADS // N_KV_HEADS
Q_W = N_Q_HEADS * HEAD_DIM
KV_W = N_KV_HEADS * HEAD_DIM
QKV_W = Q_W + 2 * KV_W
ROPE_THETA = 10000.0
POOL_HALF = (1, 2, 4, 8)
CHUNK = 128
GMLP_W = 2 * D_MODEL
GMLP_GROUPS = 4
GMLP_GW = GMLP_W // GMLP_GROUPS
D_FF = ((8 * D_MODEL // 3 + 127) // 128) * 128
ALPHA = (2 * DEPTH) ** 0.25
LN_EPS = 1e-6

N_P = BATCH * SEQ
N_S = DEC_BATCH * DEC_SEQ
N_TOK = N_P + N_S
N_COND = 1 + DEC_BATCH
COND_PAD = 8

F32 = jnp.float32
BF16 = jnp.bfloat16

VMEM_LIMIT_BYTES = 56 * 1024 * 1024
BF16_ROWS = 16

TM_SEQ = SEQ
TM_FFN = 512
TM_QKV = 256
TM_GMLP = 512
LANES = 128
MXU_DIM = 256
FFN_CHUNKS = (5 * MXU_DIM, 6 * MXU_DIM)
assert sum(FFN_CHUNKS) == D_FF
DFT_RADIX = 8
DFT_SUB = DEC_SEQ // DFT_RADIX
DFT_COEF_ROWS = 128
DFT_CH = 512
ATTN_PAIR = 2


def _dot(a, b):
    return jnp.dot(a, b, preferred_element_type=F32)


def _dot_nt(a, b):
    return lax.dot_general(a, b, (((1,), (1,)), ((), ())), preferred_element_type=F32)


def _ln(x):
    mu = jnp.mean(x, axis=-1, keepdims=True)
    xc = x - mu
    var = jnp.mean(xc * xc, axis=-1, keepdims=True)
    return xc * lax.rsqrt(var + LN_EPS)


def _post_norm(x, y, gate, g, b):
    return _ln(ALPHA * x + gate * y) * g + b


def _cond_row(row_start):
    return jnp.maximum(row_start - (N_P - DEC_SEQ), 0) // DEC_SEQ


def _seq_pos(row_start):
    is_p = row_start < N_P
    length = jnp.where(is_p, SEQ, DEC_SEQ)
    pos = jnp.where(is_p, row_start, row_start - N_P) & (length - 1)
    return length, pos


def _params():
    return pltpu.CompilerParams(dimension_semantics=("arbitrary",), vmem_limit_bytes=VMEM_LIMIT_BYTES)


def _resident(shape):
    return pl.BlockSpec(shape, lambda i: (0,) * len(shape), pipeline_mode=pl.Buffered(1))


def _layer(tail, layer):
    return pl.BlockSpec((None,) + tuple(tail), lambda i: (layer,) + (0,) * len(tail),
                        pipeline_mode=pl.Buffered(1))


def _mod_spec(tm, layer):
    return pl.BlockSpec((None, None, 6, D_MODEL), lambda i: (layer, _cond_row(i * tm), 0, 0))


def _ln_spec(layer, which):
    return pl.BlockSpec((None, 1, D_MODEL), lambda i: (2 * layer + which, 0, 0),
                        pipeline_mode=pl.Buffered(1))


def _row_spec(tm, width=D_MODEL):
    return pl.BlockSpec((tm, width), lambda i: (i, 0))


def _stream_specs(tm, width=D_MODEL):
    n_pt = N_P // tm
    return (pl.BlockSpec((tm, width), lambda i: (jnp.minimum(i, n_pt - 1), 0)),
            pl.BlockSpec((tm, width), lambda i: (jnp.maximum(i - n_pt, 0), 0)))


CAST_STEPS = 16


def _cast_plan(w, first_layer, n_layers, first_step, transform=None):
    _, rows, width = w.shape
    rb = rows // CAST_STEPS
    assert rb * CAST_STEPS == rows and rb % BF16_ROWS == 0

    def block(i):
        j = jnp.clip(i - first_step, 0, n_layers * CAST_STEPS - 1)
        return j // CAST_STEPS, j % CAST_STEPS

    in_spec = pl.BlockSpec((None, rb, width), lambda i: (first_layer + block(i)[0], block(i)[1], 0))
    out_spec = pl.BlockSpec((None, rb, width), lambda i: (block(i)[0], block(i)[1], 0))
    job = (transform, first_step, first_step + n_layers * CAST_STEPS)
    return w, in_spec, out_spec, jax.ShapeDtypeStruct((n_layers, rows, width), BF16), job


def _run_casts(src_refs, dst_refs, jobs):
    i = pl.program_id(0)
    for src, dst, (fn, lo, hi) in zip(src_refs, dst_refs, jobs, strict=True):
        @pl.when((i >= lo) & (i < hi))
        def _():
            v = src[...]
            dst[...] = (v if fn is None else fn(v)).astype(BF16)


def _swap_mid_quarters(v):
    q = HEAD_DIM // 4
    lane = lax.broadcasted_iota(jnp.int32, (1, HEAD_DIM), 1)
    out = []
    for h in range(v.shape[1] // HEAD_DIM):
        x = v[:, h * HEAD_DIM:(h + 1) * HEAD_DIM]
        x = jnp.where((lane >= q) & (lane < 2 * q), pltpu.roll(x, HEAD_DIM - q, 1),
                      jnp.where((lane >= 2 * q) & (lane < 3 * q), pltpu.roll(x, q, 1), x))
        out.append(x)
    return jnp.concatenate(out, axis=1) if len(out) > 1 else out[0]


def _qk_cols_to_head_perm(v):
    return jnp.concatenate([_swap_mid_quarters(v[:, :Q_W + KV_W]), v[:, Q_W + KV_W:]], axis=1)


def _mod_kernel(c_ref, w_ref, b_ref, o_ref):
    c = c_ref[...]
    s = c / (1.0 + jnp.exp(-c))
    o_ref[...] = _dot(s.astype(BF16), w_ref[...].astype(BF16)) + b_ref[...]


def _modulation(cond, w_mod, b_mod):
    tn = 2048
    out = pl.pallas_call(
        _mod_kernel,
        grid=(DEPTH, 6 * D_MODEL // tn),
        in_specs=[
            pl.BlockSpec((COND_PAD, D_MODEL), lambda l, j: (0, 0)),
            pl.BlockSpec((None, D_MODEL, tn), lambda l, j: (l, 0, j)),
            pl.BlockSpec((None, 1, tn), lambda l, j: (l, 0, j)),
        ],
        out_specs=pl.BlockSpec((None, COND_PAD, tn), lambda l, j: (l, 0, j)),
        out_shape=jax.ShapeDtypeStruct((DEPTH, COND_PAD, 6 * D_MODEL), F32),
        compiler_params=pltpu.CompilerParams(
            dimension_semantics=("arbitrary", "arbitrary"), vmem_limit_bytes=VMEM_LIMIT_BYTES),
        name="modulation",
    )(cond, w_mod, b_mod.reshape(DEPTH, 1, 6 * D_MODEL))
    return out[:, :N_COND].reshape(DEPTH, N_COND, 6, D_MODEL)


def _ffn_kernel(x_ref, xp_ref, xn_ref, mod_ref, g_ref, b_ref, wup_ref, cw_ref, cb_ref, wdn_ref,
                *rest, tm, split_out):
    halo = BF16_ROWS
    out_refs, (hext_ref, aext_ref) = rest[:-2], rest[-2:]
    start = pl.program_id(0) * tm
    length, pos0 = _seq_pos(start)
    m = mod_ref[...]
    shift, scale, gate = m[3:4], m[4:5], m[5:6]

    def modulate(v):
        return (_ln(v) * (1.0 + scale) + shift).astype(BF16)

    x = x_ref[...]
    hext_ref[0:halo] = modulate(xp_ref[...])
    hext_ref[halo:halo + tm] = modulate(x)
    hext_ref[halo + tm:] = modulate(xn_ref[...])

    t = (pos0 + lax.broadcasted_iota(jnp.int32, (tm, 1), 0)) & (length - 1)
    has_prev = t != 0
    has_next = t != length - 1

    acc = jnp.zeros((tm, D_MODEL), F32)
    lo = 0
    for fc in FFN_CHUNKS:
        aext_ref[:, 0:fc] = _dot(hext_ref[...], wup_ref[:, lo:lo + fc])
        bgate = _dot(hext_ref[halo:halo + tm], wup_ref[:, D_FF + lo:D_FF + lo + fc])
        cw = cw_ref[:, lo:lo + fc]
        a = (jnp.where(has_prev, aext_ref[halo - 1:halo - 1 + tm, 0:fc], 0.0) * cw[0:1]
             + aext_ref[halo:halo + tm, 0:fc] * cw[1:2]
             + jnp.where(has_next, aext_ref[halo + 1:halo + 1 + tm, 0:fc], 0.0) * cw[2:3]
             + cb_ref[:, lo:lo + fc])
        act = (jax.nn.gelu(a) * bgate).astype(BF16)
        acc = acc + _dot(act, wdn_ref[lo:lo + fc, :])
        lo += fc
    y = _post_norm(x, acc, gate, g_ref[...], b_ref[...])

    if split_out:
        @pl.when(start < N_P)
        def _():
            out_refs[0][...] = y

        @pl.when(start >= N_P)
        def _():
            out_refs[1][...] = y
    else:
        out_refs[0][...] = y


def _ffn(x, mod, ln_g, ln_b, w_up, conv_w, conv_b, w_down, layer, w_layer, split_out):
    tm, halo = TM_FFN, BF16_ROWS
    nb = N_TOK // halo
    if split_out:
        out_specs = list(_stream_specs(tm))
        out_shape = [jax.ShapeDtypeStruct((N_P, D_MODEL), F32), jax.ShapeDtypeStruct((N_S, D_MODEL), F32)]
    else:
        out_specs = _row_spec(tm)
        out_shape = jax.ShapeDtypeStruct((N_TOK, D_MODEL), F32)
    return pl.pallas_call(
        functools.partial(_ffn_kernel, tm=tm, split_out=split_out),
        grid=(N_TOK // tm,),
        in_specs=[
            _row_spec(tm),
            pl.BlockSpec((halo, D_MODEL), lambda i: (jnp.maximum(i * (tm // halo) - 1, 0), 0)),
            pl.BlockSpec((halo, D_MODEL), lambda i: (jnp.minimum((i + 1) * (tm // halo), nb - 1), 0)),
            _mod_spec(tm, layer),
            _ln_spec(layer, 1), _ln_spec(layer, 1),
            _layer((D_MODEL, 2 * D_FF), w_layer),
            _layer((3, D_FF), layer),
            _layer((1, D_FF), layer),
            _layer((D_FF, D_MODEL), w_layer),
        ],
        out_specs=out_specs,
        out_shape=out_shape,
        scratch_shapes=[
            pltpu.VMEM((tm + 2 * halo, D_MODEL), BF16),
            pltpu.VMEM((tm + 2 * halo, max(FFN_CHUNKS)), F32),
        ],
        compiler_params=_params(),
        name="conv_ffn",
    )(x, x, x, mod, ln_g, ln_b, w_up, conv_w, conv_b, w_down)


def _dft_matrix(n):
    k = np.arange(n, dtype=np.int64)
    ang = 2.0 * np.pi * ((k[:, None] * k[None, :]) % n) / n
    return np.cos(ang).astype(np.float32), np.sin(ang).astype(np.float32)


def _dft_twiddle_tables():
    t1 = np.arange(DFT_RADIX, dtype=np.int64)[:, None]
    k2 = np.arange(DFT_SUB, dtype=np.int64)[None, :]
    ang = 2.0 * np.pi * (t1 * k2) / DEC_SEQ
    shape = (DFT_RADIX, DFT_SUB // DFT_COEF_ROWS, DFT_COEF_ROWS)
    pad = ((0, 0), (0, DFT_COEF_ROWS - shape[1]), (0, 0))
    f = lambda a: np.pad(a.reshape(shape), pad).astype(np.float32)
    return f(np.cos(ang)), f(np.sin(ang))


def _octant(n):
    r = math.sqrt(0.5)
    return [(1.0, 0.0), (r, r), (0.0, 1.0), (-r, r), (-1.0, 0.0), (-r, -r), (0.0, -1.0), (r, -r)][n % 8]


def _fnet_a_kernel(xp_ref, xs_ref, mod_ref, cs_ref, pp_ref, qp_ref, pd_ref, qd_ref, p_scr, q_scr, *, tm):
    is_prompt = pl.program_id(0) < N_P // tm
    x = jnp.where(is_prompt, xp_ref[...], xs_ref[...])
    m = mod_ref[...]
    h = (_ln(x) * (1.0 + m[1:2]) + m[0:1]).astype(BF16)
    per_group = GROUP_W // LANES
    for g in range(N_GROUPS):
        r = _dot(h[:, g * GROUP_W:(g + 1) * GROUP_W], cs_ref[...])
        for c in range(per_group):
            p_scr[g * per_group + c] = r[:, c * LANES:(c + 1) * LANES]
            q_scr[g * per_group + c] = r[:, GROUP_W + c * LANES:GROUP_W + (c + 1) * LANES]

    @pl.when(is_prompt)
    def _():
        for c in range(D_MODEL // LANES):
            pp_ref[:, c * LANES:(c + 1) * LANES] = p_scr[c].astype(BF16)
            qp_ref[:, c * LANES:(c + 1) * LANES] = q_scr[c].astype(BF16)

    @pl.when(jnp.logical_not(is_prompt))
    def _():
        for t1 in range(DFT_RADIX):
            rows = pl.ds(t1, tm // DFT_RADIX, stride=DFT_RADIX)
            for c in range(D_MODEL // LANES):
                pd_ref[t1, :, c * LANES:(c + 1) * LANES] = p_scr[c, rows, :].astype(BF16)
                qd_ref[t1, :, c * LANES:(c + 1) * LANES] = q_scr[c, rows, :].astype(BF16)


def _fnet_latent_kernel(pd_ref, qd_ref, cs_ref, ct_ref, st_ref, f_ref, a_ref, b_ref):
    for t1 in range(DFT_RADIX):
        csp = _dot(cs_ref[...], pd_ref[t1])
        csq = _dot(cs_ref[...], qd_ref[t1])
        u = csp[:DFT_SUB] - csq[DFT_SUB:]
        v = csq[:DFT_SUB] + csp[DFT_SUB:]
        if t1 == 0:
            a_ref[0], b_ref[0] = u, v
            continue
        ct, st = ct_ref[t1].T, st_ref[t1].T
        for blk in range(DFT_SUB // DFT_COEF_ROWS):
            rows = slice(blk * DFT_COEF_ROWS, (blk + 1) * DFT_COEF_ROWS)
            cc, ss = ct[:, blk:blk + 1], st[:, blk:blk + 1]
            a_ref[t1, rows, :] = cc * u[rows] - ss * v[rows]
            b_ref[t1, rows, :] = ss * u[rows] + cc * v[rows]

    norm = (DEC_SEQ * GROUP_W) ** -0.5
    for k1 in range(DFT_RADIX):
        for blk in range(DFT_SUB // DFT_COEF_ROWS):
            rows = slice(blk * DFT_COEF_ROWS, (blk + 1) * DFT_COEF_ROWS)
            acc = a_ref[0, rows, :]
            for t1 in range(1, DFT_RADIX):
                c, s = _octant(k1 * t1)
                if s == 0.0:
                    acc = acc + a_ref[t1, rows, :] if c > 0 else acc - a_ref[t1, rows, :]
                elif c == 0.0:
                    acc = acc - b_ref[t1, rows, :] if s > 0 else acc + b_ref[t1, rows, :]
                else:
                    acc = acc + (c * a_ref[t1, rows, :] - s * b_ref[t1, rows, :])
            out_rows = slice(k1 * DFT_SUB + blk * DFT_COEF_ROWS, k1 * DFT_SUB + (blk + 1) * DFT_COEF_ROWS)
            f_ref[out_rows, :] = (acc * norm).astype(BF16)


def _fnet_b_kernel(xp_ref, xs_ref, pp_ref, qp_ref, fs_ref, c256_ref, s256_ref, wo_ref, mod_ref, g_ref,
                   b_ref, *rest, tm, n_cast, cast_jobs):
    cast_src, o_ref, cast_dst, f_ref = rest[:n_cast], rest[n_cast], rest[n_cast + 1:-1], rest[-1]
    _run_casts(cast_src, cast_dst, cast_jobs)
    is_prompt = pl.program_id(0) < N_P // tm

    @pl.when(is_prompt)
    def _():
        for s in range(tm // SEQ):
            rows = slice(s * SEQ, (s + 1) * SEQ)
            f = _dot(c256_ref[...], pp_ref[rows, :]) - _dot(s256_ref[...], qp_ref[rows, :])
            f_ref[rows, :] = (f * (SEQ * GROUP_W) ** -0.5).astype(BF16)

    @pl.when(jnp.logical_not(is_prompt))
    def _():
        f_ref[...] = fs_ref[...]

    x = jnp.where(is_prompt, xp_ref[...], xs_ref[...])
    y = _dot(f_ref[...], wo_ref[...])
    o_ref[...] = _post_norm(x, y, mod_ref[...][2:3], g_ref[...], b_ref[...])


def _fnet(xp, xs, mod, ln_g, ln_b, w_o, layer, j, casts):
    c256, s256 = _dft_matrix(SEQ)
    cs = jnp.concatenate([jnp.asarray(c256), jnp.asarray(s256)], axis=1).astype(BF16)
    tm = 512
    n_pt = N_P // tm
    per_seq = DEC_SEQ // tm
    prompt_rows = _stream_specs(tm)[0]
    slabs = pl.BlockSpec((None, DFT_RADIX, tm // DFT_RADIX, D_MODEL),
                         lambda i: (jnp.maximum(i - n_pt, 0) // per_seq, 0, jnp.maximum(i - n_pt, 0) % per_seq, 0))
    dec_shape = jax.ShapeDtypeStruct((DEC_BATCH, DFT_RADIX, DFT_SUB, D_MODEL), BF16)
    pp, qp, pd, qd = pl.pallas_call(
        functools.partial(_fnet_a_kernel, tm=tm),
        grid=(N_TOK // tm,),
        in_specs=[*_stream_specs(tm), _mod_spec(tm, layer), _resident((GROUP_W, 2 * GROUP_W))],
        out_specs=[prompt_rows, prompt_rows, slabs, slabs],
        out_shape=[jax.ShapeDtypeStruct((N_P, D_MODEL), BF16)] * 2 + [dec_shape] * 2,
        scratch_shapes=[pltpu.VMEM((D_MODEL // LANES, tm, LANES), F32)] * 2,
        compiler_params=_params(),
        name="fnet_channel_dft",
    )(xp, xs, mod, cs)

    c_sub, s_sub = _dft_matrix(DFT_SUB)
    cs_sub = jnp.concatenate([jnp.asarray(c_sub), jnp.asarray(s_sub)], axis=0).astype(BF16)
    ct, st = _dft_twiddle_tables()
    slabs_in = pl.BlockSpec((None, DFT_RADIX, DFT_SUB, DFT_CH), lambda b, c: (b, 0, 0, c))
    whole = lambda shape: pl.BlockSpec(shape, lambda b, c: (0,) * len(shape), pipeline_mode=pl.Buffered(1))
    f_s = pl.pallas_call(
        _fnet_latent_kernel,
        grid=(DEC_BATCH, D_MODEL // DFT_CH),
        in_specs=[slabs_in, slabs_in, whole((2 * DFT_SUB, DFT_SUB)), whole(ct.shape), whole(st.shape)],
        out_specs=pl.BlockSpec((DEC_SEQ, DFT_CH), lambda b, c: (b, c)),
        out_shape=jax.ShapeDtypeStruct((N_S, D_MODEL), BF16),
        scratch_shapes=[pltpu.VMEM((DFT_RADIX, DFT_SUB, DFT_CH), F32)] * 2,
        compiler_params=pltpu.CompilerParams(
            dimension_semantics=("arbitrary",) * 2, vmem_limit_bytes=VMEM_LIMIT_BYTES),
        name="fnet_latent_dft",
    )(pd, qd, cs_sub, jnp.asarray(ct), jnp.asarray(st))

    prompt_rows, latent_rows = _stream_specs(tm)
    return pl.pallas_call(
        functools.partial(_fnet_b_kernel, tm=tm, n_cast=len(casts), cast_jobs=tuple(c[4] for c in casts)),
        grid=(N_TOK // tm,),
        in_specs=[
            *_stream_specs(tm),
            prompt_rows, prompt_rows, latent_rows,
            _resident((SEQ, SEQ)), _resident((SEQ, SEQ)),
            _layer((D_MODEL, D_MODEL), j),
            _mod_spec(tm, layer),
            _ln_spec(layer, 0), _ln_spec(layer, 0),
            *[c[1] for c in casts],
        ],
        out_specs=[_row_spec(tm), *[c[2] for c in casts]],
        out_shape=[jax.ShapeDtypeStruct((N_TOK, D_MODEL), F32), *[c[3] for c in casts]],
        scratch_shapes=[pltpu.VMEM((tm, D_MODEL), BF16)],
        compiler_params=_params(),
        name="fnet_token_dft",
    )(xp, xs, pp, qp, f_s, jnp.asarray(c256).astype(BF16), jnp.asarray(s256).astype(BF16), w_o, mod,
      ln_g, ln_b, *[c[0] for c in casts])


def _head_perm():
    quarter = HEAD_DIM // 4
    d = np.arange(HEAD_DIM).reshape(4, quarter)
    return d[[0, 2, 1, 3]].reshape(HEAD_DIM)


def _to_head_perm(a):
    q = HEAD_DIM // 4
    return jnp.concatenate([a[..., :q], a[..., 2 * q:3 * q], a[..., q:2 * q], a[..., 3 * q:]], axis=-1)


def _rope_tables(tm):
    quarter = HEAD_DIM // 4
    t = np.arange(DEC_SEQ)
    row, col = (t // GRID_W).astype(np.float32), (t % GRID_W).astype(np.float32)
    inv = (np.float32(ROPE_THETA) ** (-np.arange(quarter, dtype=np.float32) / np.float32(quarter))).astype(np.float32)
    ang_r = (row[:, None] * inv[None, :]).astype(np.float32).astype(np.float64)
    ang_c = (col[:, None] * inv[None, :]).astype(np.float32).astype(np.float64)
    cos = np.concatenate([np.cos(ang_r), np.cos(ang_r), np.cos(ang_c), np.cos(ang_c)], axis=1)
    sin = np.concatenate([-np.sin(ang_r), np.sin(ang_r), -np.sin(ang_c), np.sin(ang_c)], axis=1)
    cos = np.concatenate([cos, np.ones((tm, HEAD_DIM))], axis=0)[:, _head_perm()]
    sin = np.concatenate([sin, np.zeros((tm, HEAD_DIM))], axis=0)[:, _head_perm()]
    return np.asarray(cos, np.float32), np.asarray(sin, np.float32)


def _qkv_kernel(x_ref, mod_ref, w_ref, qg_ref, kg_ref, cos_ref, sin_ref,
                q_ref, k_ref, v_ref, vt_ref, nk_ref, nv_ref, *, tm):
    i = pl.program_id(0)
    m = mod_ref[...]
    h = (_ln(x_ref[...]) * (1.0 + m[1:2]) + m[0:1]).astype(BF16)
    cos, sin = cos_ref[...], sin_ref[...]

    def rms(v, g):
        return v * lax.rsqrt(jnp.mean(v * v, axis=-1, keepdims=True) + LN_EPS) * g

    def rope(v):
        return v * cos + pltpu.roll(v, HEAD_DIM // 2, 1) * sin

    def project(lo):
        return _dot(h, w_ref[:, lo:lo + MXU_DIM])

    heads_per_block = MXU_DIM // HEAD_DIM
    for blk in range(Q_W // MXU_DIM):
        qq = project(blk * MXU_DIM)
        for j in range(heads_per_block):
            sl = slice((blk * heads_per_block + j) * HEAD_DIM, (blk * heads_per_block + j + 1) * HEAD_DIM)
            q_ref[:, sl] = rope(rms(qq[:, j * HEAD_DIM:(j + 1) * HEAD_DIM], qg_ref[...])).astype(BF16)
    kk = project(Q_W)
    kn = [rms(kk[:, hk * HEAD_DIM:(hk + 1) * HEAD_DIM], kg_ref[...]) for hk in range(N_KV_HEADS)]
    for hk in range(N_KV_HEADS):
        k_ref[:, hk * HEAD_DIM:(hk + 1) * HEAD_DIM] = rope(kn[hk]).astype(BF16)
    v = project(Q_W + KV_W)
    v_ref[...] = v.astype(BF16)

    is_prompt = i < N_P // tm

    @pl.when(is_prompt)
    def _():
        for hk in range(N_KV_HEADS):
            nk_ref[:, hk * HEAD_DIM:(hk + 1) * HEAD_DIM] = _swap_mid_quarters(kn[hk])
        nv_ref[...] = v

    @pl.when(jnp.logical_not(is_prompt))
    def _():
        vt_ref[...] = v.T.astype(BF16)


def _attn_kernel(x_ref, q_ref, ks_ref, vs_ref, kb_ref, vbt_ref, kc_ref, vct_ref, wo_ref, mod_ref, g_ref,
                 b_ref, *rest, n_cast, cast_jobs):
    cast_src, o_ref, cast_dst, att_ref = rest[:n_cast], rest[n_cast], rest[n_cast + 1:-1], rest[-1]
    _run_casts(cast_src, cast_dst, cast_jobs)
    i = pl.program_id(0)
    tm = TM_SEQ
    is_prompt = i < N_P // tm

    def head(hq):
        return slice(hq * HEAD_DIM, (hq + 1) * HEAD_DIM)

    @pl.when(is_prompt)
    def _():
        for hk in range(N_KV_HEADS):
            q4 = jnp.concatenate([q_ref[:, head(hk * Q_PER_KV + g)] for g in range(Q_PER_KV)], axis=0)
            s = _dot_nt(q4, ks_ref[:, head(hk)])
            p = jnp.exp2(s - jnp.max(s, axis=-1, keepdims=True))
            o4 = _dot(p.astype(BF16), vs_ref[:, head(hk)]) / jnp.sum(p, axis=-1, keepdims=True)
            for g in range(Q_PER_KV):
                att_ref[:, head(hk * Q_PER_KV + g)] = o4[g * tm:(g + 1) * tm].astype(BF16)

    @pl.when(jnp.logical_not(is_prompt))
    def _():
        for hk in range(N_KV_HEADS):
            for g0 in range(0, Q_PER_KV, ATTN_PAIR):
                hq0 = hk * Q_PER_KV + g0
                qq = jnp.concatenate([q_ref[:, head(hq0 + g)] for g in range(ATTN_PAIR)], axis=0)
                st_b = _dot_nt(kb_ref[:, head(hk)], qq)
                st_c = _dot_nt(kc_ref[:, head(hk)], qq)
                mx = jnp.maximum(jnp.max(st_b, axis=0, keepdims=True), jnp.max(st_c, axis=0, keepdims=True))
                pt_b = jnp.exp2(st_b - mx)
                pt_c = jnp.exp2(st_c - mx)
                den = jnp.sum(pt_b, axis=0, keepdims=True) + jnp.sum(pt_c, axis=0, keepdims=True)
                ot = (_dot(vbt_ref[head(hk), :], pt_b.astype(BF16))
                      + _dot(vct_ref[head(hk), :], pt_c.astype(BF16))) / den
                oo = ot.T
                for g in range(ATTN_PAIR):
                    att_ref[:, head(hq0 + g)] = oo[g * tm:(g + 1) * tm].astype(BF16)

    y = _dot(att_ref[...], wo_ref[...])
    o_ref[...] = _post_norm(x_ref[...], y, mod_ref[...][2:3], g_ref[...], b_ref[...])


def _attention(x, mod, ln_g, ln_b, w_qkv, q_norm, k_norm, w_o, cache_k, cache_v, layer, j, casts):
    tm = TM_QKV
    n_pt = N_P // tm
    cos, sin = _rope_tables(tm)
    rope_spec = pl.BlockSpec(
        (tm, HEAD_DIM),
        lambda i: (jnp.where(i < n_pt, DEC_SEQ // tm, jnp.maximum(i - n_pt, 0) % (DEC_SEQ // tm)), 0))
    prompt_row = _stream_specs(tm, KV_W)[0]
    latent_col = pl.BlockSpec((KV_W, tm), lambda i: (0, jnp.maximum(i - n_pt, 0)))
    q_gain = _to_head_perm(q_norm[j] * (math.log2(math.e) * HEAD_DIM ** -0.5))
    k_gain = _to_head_perm(k_norm[j])
    q, k, v, vt, new_k, new_v = pl.pallas_call(
        functools.partial(_qkv_kernel, tm=tm),
        grid=(N_TOK // tm,),
        in_specs=[
            _row_spec(tm),
            _mod_spec(tm, layer),
            _layer((D_MODEL, QKV_W), j),
            _resident((1, HEAD_DIM)), _resident((1, HEAD_DIM)),
            rope_spec, rope_spec,
        ],
        out_specs=[_row_spec(tm, Q_W), _row_spec(tm, KV_W), _row_spec(tm, KV_W), latent_col,
                   prompt_row, prompt_row],
        out_shape=[
            jax.ShapeDtypeStruct((N_TOK, Q_W), BF16),
            jax.ShapeDtypeStruct((N_TOK, KV_W), BF16),
            jax.ShapeDtypeStruct((N_TOK, KV_W), BF16),
            jax.ShapeDtypeStruct((KV_W, N_S), BF16),
            jax.ShapeDtypeStruct((N_P, KV_W), F32),
            jax.ShapeDtypeStruct((N_P, KV_W), F32),
        ],
        compiler_params=_params(),
        name="qkv_rope",
    )(x, mod, w_qkv, q_gain.reshape(1, HEAD_DIM), k_gain.reshape(1, HEAD_DIM), jnp.asarray(cos),
      jnp.asarray(sin))

    tm = TM_SEQ
    n_pt = N_P // tm
    per_seq = DEC_SEQ // tm
    batch_of = lambda i: jnp.maximum(i - n_pt, 0) // per_seq
    small = _stream_specs(tm, KV_W)[0]
    k_big = pl.BlockSpec((DEC_SEQ, KV_W), lambda i: (N_P // DEC_SEQ + batch_of(i), 0))
    vt_big = pl.BlockSpec((KV_W, DEC_SEQ), lambda i: (0, batch_of(i)))
    kc_spec = pl.BlockSpec((None, PAST_LEN, KV_W), lambda i: (batch_of(i), 0, 0))
    vct_spec = pl.BlockSpec((None, KV_W, PAST_LEN), lambda i: (batch_of(i), 0, 0))
    kc = _to_head_perm(cache_k[:, j]).reshape(DEC_BATCH, PAST_LEN, KV_W).astype(BF16)
    vct = cache_v[:, j].reshape(DEC_BATCH, PAST_LEN, KV_W).transpose(0, 2, 1).astype(BF16)
    x_new, *cast_out = pl.pallas_call(
        functools.partial(_attn_kernel, n_cast=len(casts), cast_jobs=tuple(c[4] for c in casts)),
        grid=(N_TOK // tm,),
        in_specs=[
            _row_spec(tm), _row_spec(tm, Q_W),
            small, small, k_big, vt_big, kc_spec, vct_spec,
            _layer((Q_W, D_MODEL), j),
            _mod_spec(tm, layer),
            _ln_spec(layer, 0), _ln_spec(layer, 0),
            *[c[1] for c in casts],
        ],
        out_specs=[_row_spec(tm), *[c[2] for c in casts]],
        out_shape=[jax.ShapeDtypeStruct((N_TOK, D_MODEL), F32), *[c[3] for c in casts]],
        scratch_shapes=[pltpu.VMEM((tm, Q_W), BF16)],
        compiler_params=_params(),
        name="attention",
    )(x, q, k, v, k, vt, kc, vct, w_o, mod, ln_g, ln_b, *[c[0] for c in casts])
    return (x_new, new_k, new_v, *cast_out)


POOL_HALO = 8


def _pool_kernel(x_ref, xp_ref, xn_ref, mod_ref, w_ref, sc_ref, g_ref, b_ref, o_ref, buf_ref, p_ref):
    tm, halo = TM_SEQ, POOL_HALO
    start = pl.program_id(0) * tm
    length, pos0 = _seq_pos(start)
    m = mod_ref[...]
    shift, scale, gate = m[0:1], m[1:2], m[2:3]

    def modulate(v):
        return _ln(v) * (1.0 + scale) + shift

    x = x_ref[...]
    buf_ref[0:halo] = jnp.where(pos0 != 0, modulate(xp_ref[...]), 0.0)
    buf_ref[halo:halo + tm] = modulate(x)
    buf_ref[halo + tm:] = jnp.where(pos0 + tm != length, modulate(xn_ref[...]), 0.0)

    t = pos0 + lax.broadcasted_iota(jnp.int32, (tm, 1), 0)
    n = tm + 2 * halo
    for g, half in enumerate(POOL_HALF):
        cols = slice(g * GROUP_W, (g + 1) * GROUP_W)
        hbuf = buf_ref[:, cols]
        wsum = hbuf + pltpu.roll(hbuf, 1, 0)
        h = 1
        while h < half:
            wsum = pltpu.roll(wsum, h, 0) + pltpu.roll(wsum, n - h, 0)
            h *= 2
        cnt = (jnp.minimum(t + half, length) - jnp.maximum(t - half, 0)).astype(F32)
        pooled = wsum[halo:halo + tm] / cnt - hbuf[halo:halo + tm]
        p_ref[:, cols] = _dot(pooled.astype(BF16), w_ref[g])
    y = p_ref[...] * sc_ref[...]
    o_ref[...] = _post_norm(x, y, gate, g_ref[...], b_ref[...])


def _pool(x, mod, ln_g, ln_b, w_grp, scale, layer, j):
    tm, halo = TM_SEQ, POOL_HALO
    nb = N_TOK // halo
    return pl.pallas_call(
        _pool_kernel,
        grid=(N_TOK // tm,),
        in_specs=[
            _row_spec(tm),
            pl.BlockSpec((halo, D_MODEL), lambda i: (jnp.maximum(i * (tm // halo) - 1, 0), 0)),
            pl.BlockSpec((halo, D_MODEL), lambda i: (jnp.minimum((i + 1) * (tm // halo), nb - 1), 0)),
            _mod_spec(tm, layer),
            _layer((N_GROUPS, GROUP_W, GROUP_W), j),
            _layer((1, D_MODEL), j),
            _ln_spec(layer, 0), _ln_spec(layer, 0),
        ],
        out_specs=_row_spec(tm),
        out_shape=jax.ShapeDtypeStruct((N_TOK, D_MODEL), F32),
        scratch_shapes=[
            pltpu.VMEM((tm + 2 * halo, D_MODEL), F32),
            pltpu.VMEM((tm, D_MODEL), F32),
        ],
        compiler_params=_params(),
        name="pool_mixer",
    )(x, x, x, mod, w_grp, scale, ln_g, ln_b)


def _gmlp_kernel(x_ref, mod_ref, win_ref, bin_ref, vg_ref, vb_ref, ws_ref, bs_ref, wo_ref, g_ref, b_ref,
                 o_ref, act_ref, v_ref, *, tm):
    m = mod_ref[...]
    x = x_ref[...]
    h = (_ln(x) * (1.0 + m[1:2]) + m[0:1]).astype(BF16)

    def branch(lo, width):
        return jax.nn.gelu(_dot(h, win_ref[:, lo:lo + width]) + bin_ref[:, lo:lo + width])

    half = GMLP_W // 2
    v = jnp.concatenate([branch(GMLP_W, half), branch(GMLP_W + half, half)], axis=1)
    v_ref[...] = (_ln(v) * vg_ref[...] + vb_ref[...]).astype(BF16)
    bs = bs_ref[...]
    groups_per_half = half // GMLP_GW
    for uc in range(GMLP_W // half):
        u = branch(uc * half, half)
        for gl in range(groups_per_half):
            g = uc * groups_per_half + gl
            cols = slice(g * GMLP_GW, (g + 1) * GMLP_GW)
            for n in range(tm // CHUNK):
                rows = slice(n * CHUNK, (n + 1) * CHUNK)
                s = _dot(ws_ref[g], v_ref[rows, cols]) + bs[:, g:g + 1]
                act_ref[rows, cols] = (u[rows, gl * GMLP_GW:(gl + 1) * GMLP_GW] * s).astype(BF16)
    y = _dot(act_ref[...], wo_ref[...])
    o_ref[...] = _post_norm(x, y, m[2:3], g_ref[...], b_ref[...])


def _gmlp(x, mod, ln_g, ln_b, w_in, b_in, v_g, v_b, w_s, b_s_t, w_o, layer, j):
    tm = TM_GMLP
    return pl.pallas_call(
        functools.partial(_gmlp_kernel, tm=tm),
        grid=(N_TOK // tm,),
        in_specs=[
            _row_spec(tm),
            _mod_spec(tm, layer),
            _layer((D_MODEL, 2 * GMLP_W), j),
            _layer((1, 2 * GMLP_W), j),
            _layer((1, GMLP_W), j), _layer((1, GMLP_W), j),
            _layer((GMLP_GROUPS, CHUNK, CHUNK), j),
            _layer((CHUNK, GMLP_GROUPS), j),
            _layer((GMLP_W, D_MODEL), j),
            _ln_spec(layer, 0), _ln_spec(layer, 0),
        ],
        out_specs=_row_spec(tm),
        out_shape=jax.ShapeDtypeStruct((N_TOK, D_MODEL), F32),
        scratch_shapes=[pltpu.VMEM((tm, GMLP_W), BF16)] * 2,
        compiler_params=_params(),
        name="gmlp_mixer",
    )(x, mod, w_in, b_in, v_g, v_b, w_s, b_s_t, w_o, ln_g, ln_b)


def kernel(x_prompt, x_sample, cache_k, cache_v, c, c_ctx, w_mod, b_mod, ln_g, ln_b, ffn_w_up, ffn_conv_w,
           ffn_conv_b, ffn_w_down, fnet_w_o, attn_w_qkv, attn_q_norm, attn_k_norm, attn_w_o, pool_w,
           pool_scale, gmlp_w_in, gmlp_b_in, gmlp_ln_g, gmlp_ln_b, gmlp_w_s, gmlp_b_s, gmlp_w_o):
    assert DEPTH == 4
    cond = jnp.concatenate([c_ctx[None, :], c, jnp.zeros((COND_PAD - N_COND, D_MODEL), F32)], axis=0)
    mod = _modulation(cond, w_mod, b_mod)
    ln_g = ln_g.reshape(DEPTH * 2, 1, D_MODEL)
    ln_b = ln_b.reshape(DEPTH * 2, 1, D_MODEL)
    conv_b = ffn_conv_b.reshape(DEPTH, 1, D_FF)

    steps_f = N_TOK // 512
    assert 2 * CAST_STEPS <= steps_f
    x, up0, down0, w_qkv, w_ao, w_pool = _fnet(
        x_prompt.reshape(N_P, D_MODEL), x_sample.reshape(N_S, D_MODEL), mod, ln_g, ln_b,
        fnet_w_o.astype(BF16), 0, 0,
        [_cast_plan(ffn_w_up, 0, 1, 0), _cast_plan(ffn_w_down, 0, 1, 0),
         _cast_plan(attn_w_qkv, 0, 1, CAST_STEPS, _qk_cols_to_head_perm),
         _cast_plan(attn_w_o, 0, 1, CAST_STEPS),
         _cast_plan(pool_w.reshape(-1, N_GROUPS * GROUP_W, GROUP_W), 0, 1, CAST_STEPS)])
    x = _ffn(x, mod, ln_g, ln_b, up0, ffn_conv_w, conv_b, down0, 0, 0, False)

    rest = DEPTH - 1
    assert (rest + 1) * CAST_STEPS <= N_TOK // TM_SEQ
    x, new_k, new_v, up, down, g_in, g_out = _attention(
        x, mod, ln_g, ln_b, w_qkv, attn_q_norm, attn_k_norm, w_ao, cache_k, cache_v, 1, 0,
        [_cast_plan(ffn_w_up, 1, rest, 0), _cast_plan(ffn_w_down, 1, rest, 0),
         _cast_plan(gmlp_w_in, 0, 1, rest * CAST_STEPS), _cast_plan(gmlp_w_o, 0, 1, rest * CAST_STEPS)])
    x = _ffn(x, mod, ln_g, ln_b, up, ffn_conv_w, conv_b, down, 1, 0, False)
    x = _pool(x, mod, ln_g, ln_b, w_pool.reshape(-1, N_GROUPS, GROUP_W, GROUP_W),
              pool_scale.reshape(-1, 1, D_MODEL), 2, 0)
    x = _ffn(x, mod, ln_g, ln_b, up, ffn_conv_w, conv_b, down, 2, 1, False)
    x = _gmlp(x, mod, ln_g, ln_b, g_in, gmlp_b_in.reshape(-1, 1, 2 * GMLP_W),
              gmlp_ln_g.reshape(-1, 1, GMLP_W), gmlp_ln_b.reshape(-1, 1, GMLP_W), gmlp_w_s.astype(BF16),
              gmlp_b_s.transpose(0, 2, 1), g_out, 3, 0)
    y_prompt, y_sample = _ffn(x, mod, ln_g, ln_b, up, ffn_conv_w, conv_b, down, 3, 2, True)

    return (y_prompt.reshape(BATCH, SEQ, D_MODEL), y_sample.reshape(DEC_BATCH, DEC_SEQ, D_MODEL),
            new_k.reshape(BATCH, 1, SEQ, N_KV_HEADS, HEAD_DIM), new_v.reshape(BATCH, 1, SEQ, N_KV_HEADS, HEAD_DIM))
```

```python
import functools
import math

import numpy as np
import jax
import jax.numpy as jnp
from jax import lax
from jax.experimental import pallas as pl
from jax.experimental.pallas import tpu as pltpu

D_MODEL = 1024
BATCH = 32
SEQ = 256
DEPTH = 4
DEC_BATCH = 2
DEC_SEQ = 4096
PAST_LEN = 512
GRID_W = 64
N_GROUPS = 4
GROUP_W = D_MODEL // N_GROUPS
HEAD_DIM = 128
N_Q_HEADS = D_MODEL // HEAD_DIM
N_KV_HEADS = 2
Q_PER_KV = N_Q_HEADS // N_KV_HEADS
Q_W = N_Q_HEADS * HEAD_DIM
KV_W = N_KV_HEADS * HEAD_DIM
QKV_W = Q_W + 2 * KV_W
ROPE_THETA = 10000.0
POOL_HALF = (1, 2, 4, 8)
CHUNK = 128
GMLP_W = 2 * D_MODEL
GMLP_GROUPS = 4
GMLP_GW = GMLP_W // GMLP_GROUPS
D_FF = ((8 * D_MODEL // 3 + 127) // 128) * 128
ALPHA = (2 * DEPTH) ** 0.25
LN_EPS = 1e-6

N_P = BATCH * SEQ
N_S = DEC_BATCH * DEC_SEQ
N_TOK = N_P + N_S
N_COND = 1 + DEC_BATCH
COND_PAD = 8

F32 = jnp.float32
BF16 = jnp.bfloat16

VMEM_LIMIT_BYTES = 56 * 1024 * 1024
BF16_ROWS = 16

TM_SEQ = SEQ
TM_FFN = 512
TM_QKV = 256
TM_GMLP = 512
LANES = 128
MXU_DIM = 256
FFN_CHUNKS = (5 * MXU_DIM, 6 * MXU_DIM)
assert sum(FFN_CHUNKS) == D_FF
DFT_RADIX = 8
DFT_SUB = DEC_SEQ // DFT_RADIX
DFT_COEF_ROWS = 128
DFT_CH = 512
ATTN_PAIR = 2


def _dot(a, b):
    return jnp.dot(a, b, preferred_element_type=F32)


def _dot_nt(a, b):
    return lax.dot_general(a, b, (((1,), (1,)), ((), ())), preferred_element_type=F32)


def _ln(x):
    mu = jnp.mean(x, axis=-1, keepdims=True)
    xc = x - mu
    var = jnp.mean(xc * xc, axis=-1, keepdims=True)
    return xc * lax.rsqrt(var + LN_EPS)


def _post_norm(x, y, gate, g, b):
    return _ln(ALPHA * x + gate * y) * g + b


def _cond_row(row_start):
    return jnp.maximum(row_start - (N_P - DEC_SEQ), 0) // DEC_SEQ


def _seq_pos(row_start):
    is_p = row_start < N_P
    length = jnp.where(is_p, SEQ, DEC_SEQ)
    pos = jnp.where(is_p, row_start, row_start - N_P) & (length - 1)
    return length, pos


def _params():
    return pltpu.CompilerParams(dimension_semantics=("arbitrary",), vmem_limit_bytes=VMEM_LIMIT_BYTES)


def _resident(shape):
    return pl.BlockSpec(shape, lambda i: (0,) * len(shape), pipeline_mode=pl.Buffered(1))


def _layer(tail, layer):
    return pl.BlockSpec((None,) + tuple(tail), lambda i: (layer,) + (0,) * len(tail),
                        pipeline_mode=pl.Buffered(1))


def _mod_spec(tm, layer):
    return pl.BlockSpec((None, None, 6, D_MODEL), lambda i: (layer, _cond_row(i * tm), 0, 0))


def _ln_spec(layer, which):
    return pl.BlockSpec((None, 1, D_MODEL), lambda i: (2 * layer + which, 0, 0),
                        pipeline_mode=pl.Buffered(1))


def _row_spec(tm, width=D_MODEL):
    return pl.BlockSpec((tm, width), lambda i: (i, 0))


def _stream_specs(tm, width=D_MODEL):
    n_pt = N_P // tm
    return (pl.BlockSpec((tm, width), lambda i: (jnp.minimum(i, n_pt - 1), 0)),
            pl.BlockSpec((tm, width), lambda i: (jnp.maximum(i - n_pt, 0), 0)))


CAST_STEPS = 16


def _cast_plan(w, first_layer, n_layers, first_step, transform=None):
    _, rows, width = w.shape
    rb = rows // CAST_STEPS
    assert rb * CAST_STEPS == rows and rb % BF16_ROWS == 0

    def block(i):
        j = jnp.clip(i - first_step, 0, n_layers * CAST_STEPS - 1)
        return j // CAST_STEPS, j % CAST_STEPS

    in_spec = pl.BlockSpec((None, rb, width), lambda i: (first_layer + block(i)[0], block(i)[1], 0))
    out_spec = pl.BlockSpec((None, rb, width), lambda i: (block(i)[0], block(i)[1], 0))
    job = (transform, first_step, first_step + n_layers * CAST_STEPS)
    return w, in_spec, out_spec, jax.ShapeDtypeStruct((n_layers, rows, width), BF16), job


def _run_casts(src_refs, dst_refs, jobs):
    i = pl.program_id(0)
    for src, dst, (fn, lo, hi) in zip(src_refs, dst_refs, jobs, strict=True):
        @pl.when((i >= lo) & (i < hi))
        def _():
            v = src[...]
            dst[...] = (v if fn is None else fn(v)).astype(BF16)


def _swap_mid_quarters(v):
    q = HEAD_DIM // 4
    lane = lax.broadcasted_iota(jnp.int32, (1, HEAD_DIM), 1)
    out = []
    for h in range(v.shape[1] // HEAD_DIM):
        x = v[:, h * HEAD_DIM:(h + 1) * HEAD_DIM]
        x = jnp.where((lane >= q) & (lane < 2 * q), pltpu.roll(x, HEAD_DIM - q, 1),
                      jnp.where((lane >= 2 * q) & (lane < 3 * q), pltpu.roll(x, q, 1), x))
        out.append(x)
    return jnp.concatenate(out, axis=1) if len(out) > 1 else out[0]


def _qk_cols_to_head_perm(v):
    return jnp.concatenate([_swap_mid_quarters(v[:, :Q_W + KV_W]), v[:, Q_W + KV_W:]], axis=1)


def _mod_kernel(c_ref, w_ref, b_ref, o_ref):
    c = c_ref[...]
    s = c / (1.0 + jnp.exp(-c))
    o_ref[...] = _dot(s.astype(BF16), w_ref[...].astype(BF16)) + b_ref[...]


def _modulation(cond, w_mod, b_mod):
    tn = 2048
    out = pl.pallas_call(
        _mod_kernel,
        grid=(DEPTH, 6 * D_MODEL // tn),
        in_specs=[
            pl.BlockSpec((COND_PAD, D_MODEL), lambda l, j: (0, 0)),
            pl.BlockSpec((None, D_MODEL, tn), lambda l, j: (l, 0, j)),
            pl.BlockSpec((None, 1, tn), lambda l, j: (l, 0, j)),
        ],
        out_specs=pl.BlockSpec((None, COND_PAD, tn), lambda l, j: (l, 0, j)),
        out_shape=jax.ShapeDtypeStruct((DEPTH, COND_PAD, 6 * D_MODEL), F32),
        compiler_params=pltpu.CompilerParams(
            dimension_semantics=("arbitrary", "arbitrary"), vmem_limit_bytes=VMEM_LIMIT_BYTES),
        name="modulation",
    )(cond, w_mod, b_mod.reshape(DEPTH, 1, 6 * D_MODEL))
    return out[:, :N_COND].reshape(DEPTH, N_COND, 6, D_MODEL)


def _ffn_kernel(x_ref, xp_ref, xn_ref, mod_ref, g_ref, b_ref, wup_ref, cw_ref, cb_ref, wdn_ref,
                *rest, tm, split_out):
    halo = BF16_ROWS
    out_refs, (hext_ref, aext_ref) = rest[:-2], rest[-2:]
    start = pl.program_id(0) * tm
    length, pos0 = _seq_pos(start)
    m = mod_ref[...]
    shift, scale, gate = m[3:4], m[4:5], m[5:6]

    def modulate(v):
        return (_ln(v) * (1.0 + scale) + shift).astype(BF16)

    x = x_ref[...]
    hext_ref[0:halo] = modulate(xp_ref[...])
    hext_ref[halo:halo + tm] = modulate(x)
    hext_ref[halo + tm:] = modulate(xn_ref[...])

    t = (pos0 + lax.broadcasted_iota(jnp.int32, (tm, 1), 0)) & (length - 1)
    has_prev = t != 0
    has_next = t != length - 1

    acc = jnp.zeros((tm, D_MODEL), F32)
    lo = 0
    for fc in FFN_CHUNKS:
        aext_ref[:, 0:fc] = _dot(hext_ref[...], wup_ref[:, lo:lo + fc])
        bgate = _dot(hext_ref[halo:halo + tm], wup_ref[:, D_FF + lo:D_FF + lo + fc])
        cw = cw_ref[:, lo:lo + fc]
        a = (jnp.where(has_prev, aext_ref[halo - 1:halo - 1 + tm, 0:fc], 0.0) * cw[0:1]
             + aext_ref[halo:halo + tm, 0:fc] * cw[1:2]
             + jnp.where(has_next, aext_ref[halo + 1:halo + 1 + tm, 0:fc], 0.0) * cw[2:3]
             + cb_ref[:, lo:lo + fc])
        act = (jax.nn.gelu(a) * bgate).astype(BF16)
        acc = acc + _dot(act, wdn_ref[lo:lo + fc, :])
        lo += fc
    y = _post_norm(x, acc, gate, g_ref[...], b_ref[...])

    if split_out:
        @pl.when(start < N_P)
        def _():
            out_refs[0][...] = y

        @pl.when(start >= N_P)
        def _():
            out_refs[1][...] = y
    else:
        out_refs[0][...] = y


def _ffn(x, mod, ln_g, ln_b, w_up, conv_w, conv_b, w_down, layer, w_layer, split_out):
    tm, halo = TM_FFN, BF16_ROWS
    nb = N_TOK // halo
    if split_out:
        out_specs = list(_stream_specs(tm))
        out_shape = [jax.ShapeDtypeStruct((N_P, D_MODEL), F32), jax.ShapeDtypeStruct((N_S, D_MODEL), F32)]
    else:
        out_specs = _row_spec(tm)
        out_shape = jax.ShapeDtypeStruct((N_TOK, D_MODEL), F32)
    return pl.pallas_call(
        functools.partial(_ffn_kernel, tm=tm, split_out=split_out),
        grid=(N_TOK // tm,),
        in_specs=[
            _row_spec(tm),
            pl.BlockSpec((halo, D_MODEL), lambda i: (jnp.maximum(i * (tm // halo) - 1, 0), 0)),
            pl.BlockSpec((halo, D_MODEL), lambda i: (jnp.minimum((i + 1) * (tm // halo), nb - 1), 0)),
            _mod_spec(tm, layer),
            _ln_spec(layer, 1), _ln_spec(layer, 1),
            _layer((D_MODEL, 2 * D_FF), w_layer),
            _layer((3, D_FF), layer),
            _layer((1, D_FF), layer),
            _layer((D_FF, D_MODEL), w_layer),
        ],
        out_specs=out_specs,
        out_shape=out_shape,
        scratch_shapes=[
            pltpu.VMEM((tm + 2 * halo, D_MODEL), BF16),
            pltpu.VMEM((tm + 2 * halo, max(FFN_CHUNKS)), F32),
        ],
        compiler_params=_params(),
        name="conv_ffn",
    )(x, x, x, mod, ln_g, ln_b, w_up, conv_w, conv_b, w_down)


def _dft_matrix(n):
    k = np.arange(n, dtype=np.int64)
    ang = 2.0 * np.pi * ((k[:, None] * k[None, :]) % n) / n
    return np.cos(ang).astype(np.float32), np.sin(ang).astype(np.float32)


def _dft_twiddle_tables():
    t1 = np.arange(DFT_RADIX, dtype=np.int64)[:, None]
    k2 = np.arange(DFT_SUB, dtype=np.int64)[None, :]
    ang = 2.0 * np.pi * (t1 * k2) / DEC_SEQ
    shape = (DFT_RADIX, DFT_SUB // DFT_COEF_ROWS, DFT_COEF_ROWS)
    pad = ((0, 0), (0, DFT_COEF_ROWS - shape[1]), (0, 0))
    f = lambda a: np.pad(a.reshape(shape), pad).astype(np.float32)
    return f(np.cos(ang)), f(np.sin(ang))


def _octant(n):
    r = math.sqrt(0.5)
    return [(1.0, 0.0), (r, r), (0.0, 1.0), (-r, r), (-1.0, 0.0), (-r, -r), (0.0, -1.0), (r, -r)][n % 8]


def _fnet_a_kernel(xp_ref, xs_ref, mod_ref, cs_ref, pp_ref, qp_ref, pd_ref, qd_ref, p_scr, q_scr, *, tm):
    is_prompt = pl.program_id(0) < N_P // tm
    x = jnp.where(is_prompt, xp_ref[...], xs_ref[...])
    m = mod_ref[...]
    h = (_ln(x) * (1.0 + m[1:2]) + m[0:1]).astype(BF16)
    per_group = GROUP_W // LANES
    for g in range(N_GROUPS):
        r = _dot(h[:, g * GROUP_W:(g + 1) * GROUP_W], cs_ref[...])
        for c in range(per_group):
            p_scr[g * per_group + c] = r[:, c * LANES:(c + 1) * LANES]
            q_scr[g * per_group + c] = r[:, GROUP_W + c * LANES:GROUP_W + (c + 1) * LANES]

    @pl.when(is_prompt)
    def _():
        for c in range(D_MODEL // LANES):
            pp_ref[:, c * LANES:(c + 1) * LANES] = p_scr[c].astype(BF16)
            qp_ref[:, c * LANES:(c + 1) * LANES] = q_scr[c].astype(BF16)

    @pl.when(jnp.logical_not(is_prompt))
    def _():
        for t1 in range(DFT_RADIX):
            rows = pl.ds(t1, tm // DFT_RADIX, stride=DFT_RADIX)
            for c in range(D_MODEL // LANES):
                pd_ref[t1, :, c * LANES:(c + 1) * LANES] = p_scr[c, rows, :].astype(BF16)
                qd_ref[t1, :, c * LANES:(c + 1) * LANES] = q_scr[c, rows, :].astype(BF16)


def _fnet_latent_kernel(pd_ref, qd_ref, cs_ref, ct_ref, st_ref, f_ref, a_ref, b_ref):
    for t1 in range(DFT_RADIX):
        csp = _dot(cs_ref[...], pd_ref[t1])
        csq = _dot(cs_ref[...], qd_ref[t1])
        u = csp[:DFT_SUB] - csq[DFT_SUB:]
        v = csq[:DFT_SUB] + csp[DFT_SUB:]
        if t1 == 0:
            a_ref[0], b_ref[0] = u, v
            continue
        ct, st = ct_ref[t1].T, st_ref[t1].T
        for blk in range(DFT_SUB // DFT_COEF_ROWS):
            rows = slice(blk * DFT_COEF_ROWS, (blk + 1) * DFT_COEF_ROWS)
            cc, ss = ct[:, blk:blk + 1], st[:, blk:blk + 1]
            a_ref[t1, rows, :] = cc * u[rows] - ss * v[rows]
            b_ref[t1, rows, :] = ss * u[rows] + cc * v[rows]

    norm = (DEC_SEQ * GROUP_W) ** -0.5
    for k1 in range(DFT_RADIX):
        for blk in range(DFT_SUB // DFT_COEF_ROWS):
            rows = slice(blk * DFT_COEF_ROWS, (blk + 1) * DFT_COEF_ROWS)
            acc = a_ref[0, rows, :]
            for t1 in range(1, DFT_RADIX):
                c, s = _octant(k1 * t1)
                if s == 0.0:
                    acc = acc + a_ref[t1, rows, :] if c > 0 else acc - a_ref[t1, rows, :]
                elif c == 0.0:
                    acc = acc - b_ref[t1, rows, :] if s > 0 else acc + b_ref[t1, rows, :]
                else:
                    acc = acc + (c * a_ref[t1, rows, :] - s * b_ref[t1, rows, :])
            out_rows = slice(k1 * DFT_SUB + blk * DFT_COEF_ROWS, k1 * DFT_SUB + (blk + 1) * DFT_COEF_ROWS)
            f_ref[out_rows, :] = (acc * norm).astype(BF16)


def _fnet_b_kernel(xp_ref, xs_ref, pp_ref, qp_ref, fs_ref, c256_ref, s256_ref, wo_ref, mod_ref, g_ref,
                   b_ref, *rest, tm, n_cast, cast_jobs):
    cast_src, o_ref, cast_dst, f_ref = rest[:n_cast], rest[n_cast], rest[n_cast + 1:-1], rest[-1]
    _run_casts(cast_src, cast_dst, cast_jobs)
    is_prompt = pl.program_id(0) < N_P // tm

    @pl.when(is_prompt)
    def _():
        for s in range(tm // SEQ):
            rows = slice(s * SEQ, (s + 1) * SEQ)
            f = _dot(c256_ref[...], pp_ref[rows, :]) - _dot(s256_ref[...], qp_ref[rows, :])
            f_ref[rows, :] = (f * (SEQ * GROUP_W) ** -0.5).astype(BF16)

    @pl.when(jnp.logical_not(is_prompt))
    def _():
        f_ref[...] = fs_ref[...]

    x = jnp.where(is_prompt, xp_ref[...], xs_ref[...])
    y = _dot(f_ref[...], wo_ref[...])
    o_ref[...] = _post_norm(x, y, mod_ref[...][2:3], g_ref[...], b_ref[...])


def _fnet(xp, xs, mod, ln_g, ln_b, w_o, layer, j, casts):
    c256, s256 = _dft_matrix(SEQ)
    cs = jnp.concatenate([jnp.asarray(c256), jnp.asarray(s256)], axis=1).astype(BF16)
    tm = 512
    n_pt = N_P // tm
    per_seq = DEC_SEQ // tm
    prompt_rows = _stream_specs(tm)[0]
    slabs = pl.BlockSpec((None, DFT_RADIX, tm // DFT_RADIX, D_MODEL),
                         lambda i: (jnp.maximum(i - n_pt, 0) // per_seq, 0, jnp.maximum(i - n_pt, 0) % per_seq, 0))
    dec_shape = jax.ShapeDtypeStruct((DEC_BATCH, DFT_RADIX, DFT_SUB, D_MODEL), BF16)
    pp, qp, pd, qd = pl.pallas_call(
        functools.partial(_fnet_a_kernel, tm=tm),
        grid=(N_TOK // tm,),
        in_specs=[*_stream_specs(tm), _mod_spec(tm, layer), _resident((GROUP_W, 2 * GROUP_W))],
        out_specs=[prompt_rows, prompt_rows, slabs, slabs],
        out_shape=[jax.ShapeDtypeStruct((N_P, D_MODEL), BF16)] * 2 + [dec_shape] * 2,
        scratch_shapes=[pltpu.VMEM((D_MODEL // LANES, tm, LANES), F32)] * 2,
        compiler_params=_params(),
        name="fnet_channel_dft",
    )(xp, xs, mod, cs)

    c_sub, s_sub = _dft_matrix(DFT_SUB)
    cs_sub = jnp.concatenate([jnp.asarray(c_sub), jnp.asarray(s_sub)], axis=0).astype(BF16)
    ct, st = _dft_twiddle_tables()
    slabs_in = pl.BlockSpec((None, DFT_RADIX, DFT_SUB, DFT_CH), lambda b, c: (b, 0, 0, c))
    whole = lambda shape: pl.BlockSpec(shape, lambda b, c: (0,) * len(shape), pipeline_mode=pl.Buffered(1))
    f_s = pl.pallas_call(
        _fnet_latent_kernel,
        grid=(DEC_BATCH, D_MODEL // DFT_CH),
        in_specs=[slabs_in, slabs_in, whole((2 * DFT_SUB, DFT_SUB)), whole(ct.shape), whole(st.shape)],
        out_specs=pl.BlockSpec((DEC_SEQ, DFT_CH), lambda b, c: (b, c)),
        out_shape=jax.ShapeDtypeStruct((N_S, D_MODEL), BF16),
        scratch_shapes=[pltpu.VMEM((DFT_RADIX, DFT_SUB, DFT_CH), F32)] * 2,
        compiler_params=pltpu.CompilerParams(
            dimension_semantics=("arbitrary",) * 2, vmem_limit_bytes=VMEM_LIMIT_BYTES),
        name="fnet_latent_dft",
    )(pd, qd, cs_sub, jnp.asarray(ct), jnp.asarray(st))

    prompt_rows, latent_rows = _stream_specs(tm)
    return pl.pallas_call(
        functools.partial(_fnet_b_kernel, tm=tm, n_cast=len(casts), cast_jobs=tuple(c[4] for c in casts)),
        grid=(N_TOK // tm,),
        in_specs=[
            *_stream_specs(tm),
            prompt_rows, prompt_rows, latent_rows,
            _resident((SEQ, SEQ)), _resident((SEQ, SEQ)),
            _layer((D_MODEL, D_MODEL), j),
            _mod_spec(tm, layer),
            _ln_spec(layer, 0), _ln_spec(layer, 0),
            *[c[1] for c in casts],
        ],
        out_specs=[_row_spec(tm), *[c[2] for c in casts]],
        out_shape=[jax.ShapeDtypeStruct((N_TOK, D_MODEL), F32), *[c[3] for c in casts]],
        scratch_shapes=[pltpu.VMEM((tm, D_MODEL), BF16)],
        compiler_params=_params(),
        name="fnet_token_dft",
    )(xp, xs, pp, qp, f_s, jnp.asarray(c256).astype(BF16), jnp.asarray(s256).astype(BF16), w_o, mod,
      ln_g, ln_b, *[c[0] for c in casts])


def _head_perm():
    quarter = HEAD_DIM // 4
    d = np.arange(HEAD_DIM).reshape(4, quarter)
    return d[[0, 2, 1, 3]].reshape(HEAD_DIM)


def _to_head_perm(a):
    q = HEAD_DIM // 4
    return jnp.concatenate([a[..., :q], a[..., 2 * q:3 * q], a[..., q:2 * q], a[..., 3 * q:]], axis=-1)


def _rope_tables(tm):
    quarter = HEAD_DIM // 4
    t = np.arange(DEC_SEQ)
    row, col = (t // GRID_W).astype(np.float32), (t % GRID_W).astype(np.float32)
    inv = (np.float32(ROPE_THETA) ** (-np.arange(quarter, dtype=np.float32) / np.float32(quarter))).astype(np.float32)
    ang_r = (row[:, None] * inv[None, :]).astype(np.float32).astype(np.float64)
    ang_c = (col[:, None] * inv[None, :]).astype(np.float32).astype(np.float64)
    cos = np.concatenate([np.cos(ang_r), np.cos(ang_r), np.cos(ang_c), np.cos(ang_c)], axis=1)
    sin = np.concatenate([-np.sin(ang_r), np.sin(ang_r), -np.sin(ang_c), np.sin(ang_c)], axis=1)
    cos = np.concatenate([cos, np.ones((tm, HEAD_DIM))], axis=0)[:, _head_perm()]
    sin = np.concatenate([sin, np.zeros((tm, HEAD_DIM))], axis=0)[:, _head_perm()]
    return np.asarray(cos, np.float32), np.asarray(sin, np.float32)


def _qkv_kernel(x_ref, mod_ref, w_ref, qg_ref, kg_ref, cos_ref, sin_ref,
                q_ref, k_ref, v_ref, vt_ref, nk_ref, nv_ref, *, tm):
    i = pl.program_id(0)
    m = mod_ref[...]
    h = (_ln(x_ref[...]) * (1.0 + m[1:2]) + m[0:1]).astype(BF16)
    cos, sin = cos_ref[...], sin_ref[...]

    def rms(v, g):
        return v * lax.rsqrt(jnp.mean(v * v, axis=-1, keepdims=True) + LN_EPS) * g

    def rope(v):
        return v * cos + pltpu.roll(v, HEAD_DIM // 2, 1) * sin

    def project(lo):
        return _dot(h, w_ref[:, lo:lo + MXU_DIM])

    heads_per_block = MXU_DIM // HEAD_DIM
    for blk in range(Q_W // MXU_DIM):
        qq = project(blk * MXU_DIM)
        for j in range(heads_per_block):
            sl = slice((blk * heads_per_block + j) * HEAD_DIM, (blk * heads_per_block + j + 1) * HEAD_DIM)
            q_ref[:, sl] = rope(rms(qq[:, j * HEAD_DIM:(j + 1) * HEAD_DIM], qg_ref[...])).astype(BF16)
    kk = project(Q_W)
    kn = [rms(kk[:, hk * HEAD_DIM:(hk + 1) * HEAD_DIM], kg_ref[...]) for hk in range(N_KV_HEADS)]
    for hk in range(N_KV_HEADS):
        k_ref[:, hk * HEAD_DIM:(hk + 1) * HEAD_DIM] = rope(kn[hk]).astype(BF16)
    v = project(Q_W + KV_W)
    v_ref[...] = v.astype(BF16)

    is_prompt = i < N_P // tm

    @pl.when(is_prompt)
    def _():
        for hk in range(N_KV_HEADS):
            nk_ref[:, hk, :] = _swap_mid_quarters(kn[hk])
            nv_ref[:, hk, :] = v[:, hk * HEAD_DIM:(hk + 1) * HEAD_DIM]

    @pl.when(jnp.logical_not(is_prompt))
    def _():
        vt_ref[...] = v.T.astype(BF16)


def _attn_kernel(x_ref, q_ref, ks_ref, vs_ref, kb_ref, vbt_ref, kc_ref, vct_ref, wo_ref, mod_ref, g_ref,
                 b_ref, *rest, n_cast, cast_jobs):
    cast_src, o_ref, cast_dst, att_ref = rest[:n_cast], rest[n_cast], rest[n_cast + 1:-1], rest[-1]
    _run_casts(cast_src, cast_dst, cast_jobs)
    i = pl.program_id(0)
    tm = TM_SEQ
    is_prompt = i < N_P // tm

    def head(hq):
        return slice(hq * HEAD_DIM, (hq + 1) * HEAD_DIM)

    @pl.when(is_prompt)
    def _():
        for hk in range(N_KV_HEADS):
            q4 = jnp.concatenate([q_ref[:, head(hk * Q_PER_KV + g)] for g in range(Q_PER_KV)], axis=0)
            s = _dot_nt(q4, ks_ref[:, head(hk)])
            p = jnp.exp2(s - jnp.max(s, axis=-1, keepdims=True))
            o4 = _dot(p.astype(BF16), vs_ref[:, head(hk)]) / jnp.sum(p, axis=-1, keepdims=True)
            for g in range(Q_PER_KV):
                att_ref[:, head(hk * Q_PER_KV + g)] = o4[g * tm:(g + 1) * tm].astype(BF16)

    @pl.when(jnp.logical_not(is_prompt))
    def _():
        for hk in range(N_KV_HEADS):
            for g0 in range(0, Q_PER_KV, ATTN_PAIR):
                hq0 = hk * Q_PER_KV + g0
                qq = jnp.concatenate([q_ref[:, head(hq0 + g)] for g in range(ATTN_PAIR)], axis=0)
                st_b = _dot_nt(kb_ref[:, head(hk)], qq)
                st_c = _dot_nt(kc_ref[:, head(hk)], qq)
                mx = jnp.maximum(jnp.max(st_b, axis=0, keepdims=True), jnp.max(st_c, axis=0, keepdims=True))
                pt_b = jnp.exp2(st_b - mx)
                pt_c = jnp.exp2(st_c - mx)
                den = jnp.sum(pt_b, axis=0, keepdims=True) + jnp.sum(pt_c, axis=0, keepdims=True)
                ot = (_dot(vbt_ref[head(hk), :], pt_b.astype(BF16))
                      + _dot(vct_ref[head(hk), :], pt_c.astype(BF16))) / den
                oo = ot.T
                for g in range(ATTN_PAIR):
                    att_ref[:, head(hq0 + g)] = oo[g * tm:(g + 1) * tm].astype(BF16)

    y = _dot(att_ref[...], wo_ref[...])
    o_ref[...] = _post_norm(x_ref[...], y, mod_ref[...][2:3], g_ref[...], b_ref[...])


def _attention(x, mod, ln_g, ln_b, w_qkv, q_norm, k_norm, w_o, cache_k, cache_v, layer, j, casts):
    tm = TM_QKV
    n_pt = N_P // tm
    cos, sin = _rope_tables(tm)
    rope_spec = pl.BlockSpec(
        (tm, HEAD_DIM),
        lambda i: (jnp.where(i < n_pt, DEC_SEQ // tm, jnp.maximum(i - n_pt, 0) % (DEC_SEQ // tm)), 0))
    prompt_row = pl.BlockSpec((tm, N_KV_HEADS, HEAD_DIM), lambda i: (jnp.minimum(i, n_pt - 1), 0, 0))
    latent_col = pl.BlockSpec((KV_W, tm), lambda i: (0, jnp.maximum(i - n_pt, 0)))
    q_gain = _to_head_perm(q_norm[j] * (math.log2(math.e) * HEAD_DIM ** -0.5))
    k_gain = _to_head_perm(k_norm[j])
    q, k, v, vt, new_k, new_v = pl.pallas_call(
        functools.partial(_qkv_kernel, tm=tm),
        grid=(N_TOK // tm,),
        in_specs=[
            _row_spec(tm),
            _mod_spec(tm, layer),
            _layer((D_MODEL, QKV_W), j),
            _resident((1, HEAD_DIM)), _resident((1, HEAD_DIM)),
            rope_spec, rope_spec,
        ],
        out_specs=[_row_spec(tm, Q_W), _row_spec(tm, KV_W), _row_spec(tm, KV_W), latent_col,
                   prompt_row, prompt_row],
        out_shape=[
            jax.ShapeDtypeStruct((N_TOK, Q_W), BF16),
            jax.ShapeDtypeStruct((N_TOK, KV_W), BF16),
            jax.ShapeDtypeStruct((N_TOK, KV_W), BF16),
            jax.ShapeDtypeStruct((KV_W, N_S), BF16),
            jax.ShapeDtypeStruct((N_P, N_KV_HEADS, HEAD_DIM), F32),
            jax.ShapeDtypeStruct((N_P, N_KV_HEADS, HEAD_DIM), F32),
        ],
        compiler_params=_params(),
        name="qkv_rope",
    )(x, mod, w_qkv, q_gain.reshape(1, HEAD_DIM), k_gain.reshape(1, HEAD_DIM), jnp.asarray(cos),
      jnp.asarray(sin))

    tm = TM_SEQ
    n_pt = N_P // tm
    per_seq = DEC_SEQ // tm
    batch_of = lambda i: jnp.maximum(i - n_pt, 0) // per_seq
    small = _stream_specs(tm, KV_W)[0]
    k_big = pl.BlockSpec((DEC_SEQ, KV_W), lambda i: (N_P // DEC_SEQ + batch_of(i), 0))
    vt_big = pl.BlockSpec((KV_W, DEC_SEQ), lambda i: (0, batch_of(i)))
    kc_spec = pl.BlockSpec((None, PAST_LEN, KV_W), lambda i: (batch_of(i), 0, 0))
    vct_spec = pl.BlockSpec((None, KV_W, PAST_LEN), lambda i: (batch_of(i), 0, 0))
    kc = _to_head_perm(cache_k[:, j]).reshape(DEC_BATCH, PAST_LEN, KV_W).astype(BF16)
    vct = cache_v[:, j].reshape(DEC_BATCH, PAST_LEN, KV_W).transpose(0, 2, 1).astype(BF16)
    x_new, *cast_out = pl.pallas_call(
        functools.partial(_attn_kernel, n_cast=len(casts), cast_jobs=tuple(c[4] for c in casts)),
        grid=(N_TOK // tm,),
        in_specs=[
            _row_spec(tm), _row_spec(tm, Q_W),
            small, small, k_big, vt_big, kc_spec, vct_spec,
            _layer((Q_W, D_MODEL), j),
            _mod_spec(tm, layer),
            _ln_spec(layer, 0), _ln_spec(layer, 0),
            *[c[1] for c in casts],
        ],
        out_specs=[_row_spec(tm), *[c[2] for c in casts]],
        out_shape=[jax.ShapeDtypeStruct((N_TOK, D_MODEL), F32), *[c[3] for c in casts]],
        scratch_shapes=[pltpu.VMEM((tm, Q_W), BF16)],
        compiler_params=_params(),
        name="attention",
    )(x, q, k, v, k, vt, kc, vct, w_o, mod, ln_g, ln_b, *[c[0] for c in casts])
    return (x_new, new_k, new_v, *cast_out)


POOL_HALO = 8


def _pool_kernel(x_ref, xp_ref, xn_ref, mod_ref, w_ref, sc_ref, g_ref, b_ref, o_ref, buf_ref, p_ref):
    tm, halo = TM_SEQ, POOL_HALO
    start = pl.program_id(0) * tm
    length, pos0 = _seq_pos(start)
    m = mod_ref[...]
    shift, scale, gate = m[0:1], m[1:2], m[2:3]

    def modulate(v):
        return _ln(v) * (1.0 + scale) + shift

    x = x_ref[...]
    buf_ref[0:halo] = jnp.where(pos0 != 0, modulate(xp_ref[...]), 0.0)
    buf_ref[halo:halo + tm] = modulate(x)
    buf_ref[halo + tm:] = jnp.where(pos0 + tm != length, modulate(xn_ref[...]), 0.0)

    t = pos0 + lax.broadcasted_iota(jnp.int32, (tm, 1), 0)
    n = tm + 2 * halo
    for g, half in enumerate(POOL_HALF):
        cols = slice(g * GROUP_W, (g + 1) * GROUP_W)
        hbuf = buf_ref[:, cols]
        wsum = hbuf + pltpu.roll(hbuf, 1, 0)
        h = 1
        while h < half:
            wsum = pltpu.roll(wsum, h, 0) + pltpu.roll(wsum, n - h, 0)
            h *= 2
        cnt = (jnp.minimum(t + half, length) - jnp.maximum(t - half, 0)).astype(F32)
        pooled = wsum[halo:halo + tm] / cnt - hbuf[halo:halo + tm]
        p_ref[:, cols] = _dot(pooled.astype(BF16), w_ref[g])
    y = p_ref[...] * sc_ref[...]
    o_ref[...] = _post_norm(x, y, gate, g_ref[...], b_ref[...])


def _pool(x, mod, ln_g, ln_b, w_grp, scale, layer, j):
    tm, halo = TM_SEQ, POOL_HALO
    nb = N_TOK // halo
    return pl.pallas_call(
        _pool_kernel,
        grid=(N_TOK // tm,),
        in_specs=[
            _row_spec(tm),
            pl.BlockSpec((halo, D_MODEL), lambda i: (jnp.maximum(i * (tm // halo) - 1, 0), 0)),
            pl.BlockSpec((halo, D_MODEL), lambda i: (jnp.minimum((i + 1) * (tm // halo), nb - 1), 0)),
            _mod_spec(tm, layer),
            _layer((N_GROUPS, GROUP_W, GROUP_W), j),
            _layer((1, D_MODEL), j),
            _ln_spec(layer, 0), _ln_spec(layer, 0),
        ],
        out_specs=_row_spec(tm),
        out_shape=jax.ShapeDtypeStruct((N_TOK, D_MODEL), F32),
        scratch_shapes=[
            pltpu.VMEM((tm + 2 * halo, D_MODEL), F32),
            pltpu.VMEM((tm, D_MODEL), F32),
        ],
        compiler_params=_params(),
        name="pool_mixer",
    )(x, x, x, mod, w_grp, scale, ln_g, ln_b)


def _gmlp_kernel(x_ref, mod_ref, win_ref, bin_ref, vg_ref, vb_ref, ws_ref, bs_ref, wo_ref, g_ref, b_ref,
                 o_ref, act_ref, v_ref, *, tm):
    m = mod_ref[...]
    x = x_ref[...]
    h = (_ln(x) * (1.0 + m[1:2]) + m[0:1]).astype(BF16)

    def branch(lo, width):
        return jax.nn.gelu(_dot(h, win_ref[:, lo:lo + width]) + bin_ref[:, lo:lo + width])

    half = GMLP_W // 2
    v = jnp.concatenate([branch(GMLP_W, half), branch(GMLP_W + half, half)], axis=1)
    v_ref[...] = (_ln(v) * vg_ref[...] + vb_ref[...]).astype(BF16)
    bs = bs_ref[...]
    groups_per_half = half // GMLP_GW
    for uc in range(GMLP_W // half):
        u = branch(uc * half, half)
        for gl in range(groups_per_half):
            g = uc * groups_per_half + gl
            cols = slice(g * GMLP_GW, (g + 1) * GMLP_GW)
            for n in range(tm // CHUNK):
                rows = slice(n * CHUNK, (n + 1) * CHUNK)
                s = _dot(ws_ref[g], v_ref[rows, cols]) + bs[:, g:g + 1]
                act_ref[rows, cols] = (u[rows, gl * GMLP_GW:(gl + 1) * GMLP_GW] * s).astype(BF16)
    y = _dot(act_ref[...], wo_ref[...])
    o_ref[...] = _post_norm(x, y, m[2:3], g_ref[...], b_ref[...])


def _gmlp(x, mod, ln_g, ln_b, w_in, b_in, v_g, v_b, w_s, b_s_t, w_o, layer, j):
    tm = TM_GMLP
    return pl.pallas_call(
        functools.partial(_gmlp_kernel, tm=tm),
        grid=(N_TOK // tm,),
        in_specs=[
            _row_spec(tm),
            _mod_spec(tm, layer),
            _layer((D_MODEL, 2 * GMLP_W), j),
            _layer((1, 2 * GMLP_W), j),
            _layer((1, GMLP_W), j), _layer((1, GMLP_W), j),
            _layer((GMLP_GROUPS, CHUNK, CHUNK), j),
            _layer((CHUNK, GMLP_GROUPS), j),
            _layer((GMLP_W, D_MODEL), j),
            _ln_spec(layer, 0), _ln_spec(layer, 0),
        ],
        out_specs=_row_spec(tm),
        out_shape=jax.ShapeDtypeStruct((N_TOK, D_MODEL), F32),
        scratch_shapes=[pltpu.VMEM((tm, GMLP_W), BF16)] * 2,
        compiler_params=_params(),
        name="gmlp_mixer",
    )(x, mod, w_in, b_in, v_g, v_b, w_s, b_s_t, w_o, ln_g, ln_b)


def kernel(x_prompt, x_sample, cache_k, cache_v, c, c_ctx, w_mod, b_mod, ln_g, ln_b, ffn_w_up, ffn_conv_w,
           ffn_conv_b, ffn_w_down, fnet_w_o, attn_w_qkv, attn_q_norm, attn_k_norm, attn_w_o, pool_w,
           pool_scale, gmlp_w_in, gmlp_b_in, gmlp_ln_g, gmlp_ln_b, gmlp_w_s, gmlp_b_s, gmlp_w_o):
    assert DEPTH == 4
    cond = jnp.concatenate([c_ctx[None, :], c, jnp.zeros((COND_PAD - N_COND, D_MODEL), F32)], axis=0)
    mod = _modulation(cond, w_mod, b_mod)
    ln_g = ln_g.reshape(DEPTH * 2, 1, D_MODEL)
    ln_b = ln_b.reshape(DEPTH * 2, 1, D_MODEL)
    conv_b = ffn_conv_b.reshape(DEPTH, 1, D_FF)

    steps_f = N_TOK // 512
    assert 2 * CAST_STEPS <= steps_f
    x, up0, down0, w_qkv, w_ao, w_pool = _fnet(
        x_prompt.reshape(N_P, D_MODEL), x_sample.reshape(N_S, D_MODEL), mod, ln_g, ln_b,
        fnet_w_o.astype(BF16), 0, 0,
        [_cast_plan(ffn_w_up, 0, 1, 0), _cast_plan(ffn_w_down, 0, 1, 0),
         _cast_plan(attn_w_qkv, 0, 1, CAST_STEPS, _qk_cols_to_head_perm),
         _cast_plan(attn_w_o, 0, 1, CAST_STEPS),
         _cast_plan(pool_w.reshape(-1, N_GROUPS * GROUP_W, GROUP_W), 0, 1, CAST_STEPS)])
    x = _ffn(x, mod, ln_g, ln_b, up0, ffn_conv_w, conv_b, down0, 0, 0, False)

    rest = DEPTH - 1
    assert (rest + 1) * CAST_STEPS <= N_TOK // TM_SEQ
    x, new_k, new_v, up, down, g_in, g_out = _attention(
        x, mod, ln_g, ln_b, w_qkv, attn_q_norm, attn_k_norm, w_ao, cache_k, cache_v, 1, 0,
        [_cast_plan(ffn_w_up, 1, rest, 0), _cast_plan(ffn_w_down, 1, rest, 0),
         _cast_plan(gmlp_w_in, 0, 1, rest * CAST_STEPS), _cast_plan(gmlp_w_o, 0, 1, rest * CAST_STEPS)])
    x = _ffn(x, mod, ln_g, ln_b, up, ffn_conv_w, conv_b, down, 1, 0, False)
    x = _pool(x, mod, ln_g, ln_b, w_pool.reshape(-1, N_GROUPS, GROUP_W, GROUP_W),
              pool_scale.reshape(-1, 1, D_MODEL), 2, 0)
    x = _ffn(x, mod, ln_g, ln_b, up, ffn_conv_w, conv_b, down, 2, 1, False)
    x = _gmlp(x, mod, ln_g, ln_b, g_in, gmlp_b_in.reshape(-1, 1, 2 * GMLP_W),
              gmlp_ln_g.reshape(-1, 1, GMLP_W), gmlp_ln_b.reshape(-1, 1, GMLP_W), gmlp_w_s.astype(BF16),
              gmlp_b_s.transpose(0, 2, 1), g_out, 3, 0)
    y_prompt, y_sample = _ffn(x, mod, ln_g, ln_b, up, ffn_conv_w, conv_b, down, 3, 2, True)

    return (y_prompt.reshape(BATCH, SEQ, D_MODEL), y_sample.reshape(DEC_BATCH, DEC_SEQ, D_MODEL),
            new_k.reshape(BATCH, 1, SEQ, N_KV_HEADS, HEAD_DIM), new_v.reshape(BATCH, 1, SEQ, N_KV_HEADS, HEAD_DIM))
```
